```python
import math
import jax, jax.numpy as jnp
from jax import lax
import numpy as np

D_MODEL = 2048
BATCH = 2
SEQ = 16384
DEPTH = 2

CHUNK = 64
W_A = D_MODEL
H_A = 16
BH_A = W_A // H_A
LRU_C = 8.0
CONV_K = 4
W_B = D_MODEL
P_B = 64
H_B = W_B // P_B
G_B = 4
R_B = H_B // G_B
N_B = 128
SSD_CONV_DIM = W_B + 2 * G_B * N_B
D_IN_AB = 2 * W_A + W_B + SSD_CONV_DIM + H_B
H_C = 4
DK_C = D_MODEL // 2
DV_C = D_MODEL
DKH_C = DK_C // H_C
DVH_C = DV_C // H_C
GATE_RANK = 16
GATE_TAU = 16.0
D_IN_C = 2 * DK_C + 2 * DV_C + GATE_RANK
MEM_LEN = 256
XA_HEADS = 4
XA_DH = D_MODEL // XA_HEADS
D_FF = 3 * D_MODEL
FFN_K = 3

N_EVEN = (DEPTH + 1) // 2
N_ODD = DEPTH // 2
EPS = 1e-6

kernel_name = 'hybrid_rglru_ssd_gla_streaming_encoder'


def rms_norm(x, g):
    xf = x.astype(jnp.float32)
    y = xf * lax.rsqrt(jnp.mean(xf * xf, axis=-1, keepdims=True) + EPS)
    return (y * g.astype(jnp.float32)).astype(x.dtype)


def causal_dwconv(x, w, b):
    k, c = w.shape
    y = lax.conv_general_dilated(x, w[:, None, :].astype(x.dtype), window_strides=(1,),
                                 padding=[(k - 1, 0)],
                                 dimension_numbers=('NWC', 'WIO', 'NWC'),
                                 feature_group_count=c)
    return y + b.astype(x.dtype)


def causal_mask():
    pos = jnp.arange(CHUNK)
    return pos[:, None] >= pos[None, :]


def to_chunks(t):
    b, s = t.shape[:2]
    return jnp.moveaxis(t.reshape((b, s // CHUNK, CHUNK) + t.shape[2:]), 1, 0)


def from_chunks(t):
    t = jnp.moveaxis(t, 0, 1)
    return t.reshape((t.shape[0], t.shape[1] * t.shape[2]) + t.shape[3:])


def rg_lru(x, w_a, b_a, w_x, b_x, lam):
    bsz, s, _ = x.shape
    xh = x.reshape(bsz, s, H_A, BH_A)
    r = jax.nn.sigmoid((jnp.einsum('bshi,hij->bshj', xh, w_a).reshape(bsz, s, W_A) + b_a).astype(jnp.float32))
    i = jax.nn.sigmoid((jnp.einsum('bshi,hij->bshj', xh, w_x).reshape(bsz, s, W_A) + b_x).astype(jnp.float32))
    log_a = -LRU_C * r * jax.nn.softplus(-lam.astype(jnp.float32))
    a = jnp.exp(log_a)
    u = jnp.sqrt(-jnp.expm1(2.0 * log_a)) * (i * x.astype(jnp.float32))

    def combine(left, right):
        a_l, u_l = left
        a_r, u_r = right
        return a_r * a_l, a_r * u_l + u_r

    _, h = lax.associative_scan(combine, (a, u), axis=1)
    return h.astype(x.dtype)


def ssd_chunked(xs, dt, a_neg, bm, cm):
    bsz = xs.shape[0]
    mask5 = causal_mask()[None, :, :, None, None]

    def step(state, inp):
        xc, dtc, bc, cc = inp
        cs = jnp.cumsum(dtc * a_neg, axis=1)
        seg = cs[:, :, None] - cs[:, None, :]
        lmat = jnp.exp(jnp.where(mask5, seg, -jnp.inf))
        cb = jnp.einsum('bign,bjgn->bijg', cc, bc)
        y_intra = jnp.einsum('bijg,bijgr,bjgr,bjgrp->bigrp', cb, lmat, dtc, xc)
        y_inter = jnp.einsum('bign,bgrpn->bigrp', cc, state) * jnp.exp(cs)[..., None]
        w_end = jnp.exp(cs[:, -1:] - cs) * dtc
        new_state = state * jnp.exp(cs[:, -1])[..., None, None] + \
            jnp.einsum('bjgn,bjgr,bjgrp->bgrpn', bc, w_end, xc)
        return new_state, y_intra + y_inter

    state0 = jnp.zeros((bsz, G_B, R_B, P_B, N_B), jnp.float32)
    _, ys = lax.scan(step, state0, (to_chunks(xs), to_chunks(dt), to_chunks(bm), to_chunks(cm)))
    return from_chunks(ys)


def ab_mixer(xn, w_in, lru_conv_w, lru_conv_b, lru_w_a, lru_b_a, lru_w_x, lru_b_x, lru_lambda,
             ssd_conv_w, ssd_conv_b, ssd_dt_bias, ssd_a_log, ssd_d, ssd_norm_g, w_out):
    bsz, s, _ = xn.shape
    proj = xn @ w_in
    gate_a, x_a, z_b, xbc, dt_raw = jnp.split(
        proj, [W_A, 2 * W_A, 2 * W_A + W_B, 2 * W_A + W_B + SSD_CONV_DIM], axis=-1)
    h_a = rg_lru(causal_dwconv(x_a, lru_conv_w, lru_conv_b),
                 lru_w_a, lru_b_a, lru_w_x, lru_b_x, lru_lambda)
    y_a = jax.nn.gelu(gate_a) * h_a
    xbc = jax.nn.silu(causal_dwconv(xbc, ssd_conv_w, ssd_conv_b))
    xs, bm, cm = jnp.split(xbc, [W_B, W_B + G_B * N_B], axis=-1)
    xs = xs.reshape(bsz, s, G_B, R_B, P_B).astype(jnp.float32)
    bm = bm.reshape(bsz, s, G_B, N_B).astype(jnp.float32)
    cm = cm.reshape(bsz, s, G_B, N_B).astype(jnp.float32)
    dt = jax.nn.softplus((dt_raw + ssd_dt_bias).astype(jnp.float32)).reshape(bsz, s, G_B, R_B)
    a_neg = -jnp.exp(ssd_a_log.astype(jnp.float32)).reshape(G_B, R_B)
    y = ssd_chunked(xs, dt, a_neg, bm, cm)
    y = y + ssd_d.astype(jnp.float32).reshape(G_B, R_B)[..., None] * xs
    y = y.reshape(bsz, s, W_B) * jax.nn.silu(z_b.astype(jnp.float32))
    yg = y.reshape(bsz, s, G_B, W_B // G_B)
    yg = yg * lax.rsqrt(jnp.mean(yg * yg, axis=-1, keepdims=True) + EPS)
    y_b = (yg.reshape(bsz, s, W_B) * ssd_norm_g.astype(jnp.float32)).astype(xn.dtype)
    return jnp.concatenate([y_a, y_b], axis=-1) @ w_out


def gla_chunked(q, k, v, log_alpha):
    bsz = q.shape[0]
    mask5 = causal_mask()[None, :, :, None, None]

    def step(state, inp):
        qc, kc, vc, lac = inp
        bcum = jnp.cumsum(lac, axis=1)
        diff = bcum[:, :, None] - bcum[:, None, :]
        decay = jnp.exp(jnp.where(mask5, diff, -jnp.inf))
        att = jnp.einsum('bihk,bjhk,bijhk->bhij', qc, kc, decay)
        o_intra = jnp.einsum('bhij,bjhv->bihv', att, vc)
        o_inter = jnp.einsum('bihk,bhkv->bihv', qc * jnp.exp(bcum), state)
        b_last = bcum[:, -1]
        k_dec = kc * jnp.exp(b_last[:, None] - bcum)
        new_state = jnp.exp(b_last)[..., None] * state + jnp.einsum('bjhk,bjhv->bhkv', k_dec, vc)
        return new_state, o_intra + o_inter

    state0 = jnp.zeros((bsz, H_C, DKH_C, DVH_C), jnp.float32)
    _, ys = lax.scan(step, state0, (to_chunks(q), to_chunks(k), to_chunks(v), to_chunks(log_alpha)))
    return from_chunks(ys)


def gla_mixer(xn, w_in, w_gate_up, b_gate, norm_g, w_out):
    bsz, s, _ = xn.shape
    proj = xn @ w_in
    q, k, v, r, g_low = jnp.split(
        proj, [DK_C, 2 * DK_C, 2 * DK_C + DV_C, 2 * DK_C + 2 * DV_C], axis=-1)
    log_alpha = jax.nn.log_sigmoid((g_low @ w_gate_up + b_gate).astype(jnp.float32)) / GATE_TAU
    qh = q.reshape(bsz, s, H_C, DKH_C).astype(jnp.float32) * (DKH_C ** -0.5)
    kh = k.reshape(bsz, s, H_C, DKH_C).astype(jnp.float32)
    vh = v.reshape(bsz, s, H_C, DVH_C).astype(jnp.float32)
    o = gla_chunked(qh, kh, vh, log_alpha.reshape(bsz, s, H_C, DKH_C))
    o = o * lax.rsqrt(jnp.mean(o * o, axis=-1, keepdims=True) + EPS)
    o = o.reshape(bsz, s, DV_C) * norm_g.astype(jnp.float32) * jax.nn.silu(r.astype(jnp.float32))
    return o.astype(xn.dtype) @ w_out


def cross_attention(xn, memn, w_q, w_kv, w_o):
    bsz, s, _ = xn.shape
    q = (xn @ w_q).reshape(bsz, s, XA_HEADS, XA_DH)
    k, v = jnp.split(memn @ w_kv, 2, axis=-1)
    k = k.reshape(bsz, MEM_LEN, XA_HEADS, XA_DH)
    v = v.reshape(bsz, MEM_LEN, XA_HEADS, XA_DH)
    scores = jnp.einsum('bshd,bmhd->bhsm', q, k).astype(jnp.float32) * (XA_DH ** -0.5)
    p = jax.nn.softmax(scores, axis=-1).astype(v.dtype)
    o = jnp.einsum('bhsm,bmhd->bshd', p, v).reshape(bsz, s, D_MODEL)
    return o @ w_o


def conv_ffn(xn, w_in, conv_w, conv_b, w_out):
    h = causal_dwconv(xn @ w_in, conv_w, conv_b)
    val, gate = jnp.split(h, 2, axis=-1)
    return (val * jax.nn.gelu(gate)) @ w_out


def setup_inputs(seed: int = 0) -> dict:
    key = jax.random.key(seed)
    keys = list(jax.random.split(key, 64))
    f32 = jnp.float32

    def nk():
        return keys.pop()

    def dense(shape, fan_in, scale=1.0):
        return jax.random.normal(nk(), shape, f32) * (scale * fan_in ** -0.5)

    def gain(shape):
        return 1.0 + 0.02 * jax.random.normal(nk(), shape, f32)

    def small(shape, s=0.02):
        return s * jax.random.normal(nk(), shape, f32)

    a_c = jax.random.uniform(nk(), (N_EVEN, W_A), f32, 0.9, 0.999)
    a_base = a_c ** (1.0 / LRU_C)
    lru_lambda = jnp.log(a_base) - jnp.log1p(-a_base)
    dt0 = jnp.exp(jax.random.uniform(nk(), (N_EVEN, H_B), f32, math.log(1e-3), math.log(1e-1)))
    ssd_dt_bias = dt0 + jnp.log(-jnp.expm1(-dt0))
    ssd_a_log = jnp.log(jax.random.uniform(nk(), (N_EVEN, H_B), f32, 1.0, 16.0))

    return {
        'x': jax.random.normal(nk(), (BATCH, SEQ, D_MODEL), f32),
        'mem': jax.random.normal(nk(), (BATCH, MEM_LEN, D_MODEL), f32),
        'norm_mix_g': gain((DEPTH, D_MODEL)),
        'norm_cross_g': gain((DEPTH, D_MODEL)),
        'norm_ffn_g': gain((DEPTH, D_MODEL)),
        'ab_w_in': dense((N_EVEN, D_MODEL, D_IN_AB), D_MODEL),
        'lru_conv_w': dense((N_EVEN, CONV_K, W_A), CONV_K),
        'lru_conv_b': small((N_EVEN, W_A)),
        'lru_w_a': dense((N_EVEN, H_A, BH_A, BH_A), BH_A),
        'lru_b_a': small((N_EVEN, W_A)),
        'lru_w_x': dense((N_EVEN, H_A, BH_A, BH_A), BH_A),
        'lru_b_x': small((N_EVEN, W_A)),
        'lru_lambda': lru_lambda,
        'ssd_conv_w': dense((N_EVEN, CONV_K, SSD_CONV_DIM), CONV_K),
        'ssd_conv_b': small((N_EVEN, SSD_CONV_DIM)),
        'ssd_dt_bias': ssd_dt_bias,
        'ssd_a_log': ssd_a_log,
        'ssd_d': gain((N_EVEN, H_B)),
        'ssd_norm_g': gain((N_EVEN, W_B)),
        'ab_w_out': dense((N_EVEN, W_A + W_B, D_MODEL), W_A + W_B, 0.5),
        'gla_w_in': dense((N_ODD, D_MODEL, D_IN_C), D_MODEL),
        'gla_w_gate_up': dense((N_ODD, GATE_RANK, DK_C), GATE_RANK),
        'gla_b_gate': small((N_ODD, DK_C), 0.1),
        'gla_norm_g': gain((N_ODD, DV_C)),
        'gla_w_out': dense((N_ODD, DV_C, D_MODEL), DV_C, 0.5),
        'mem_norm_g': gain((D_MODEL,)),
        'xa_w_q': dense((DEPTH, D_MODEL, D_MODEL), D_MODEL),
        'xa_w_kv': dense((DEPTH, D_MODEL, 2 * D_MODEL), D_MODEL),
        'xa_w_o': dense((DEPTH, D_MODEL, D_MODEL), D_MODEL, 0.5),
        'ffn_w_in': dense((DEPTH, D_MODEL, 2 * D_FF), D_MODEL),
        'ffn_conv_w': dense((DEPTH, FFN_K, 2 * D_FF), FFN_K),
        'ffn_conv_b': small((DEPTH, 2 * D_FF)),
        'ffn_w_out': dense((DEPTH, D_FF, D_MODEL), D_FF, 0.5),
        'final_norm_g': gain((D_MODEL,)),
    }


def reference(x, mem, norm_mix_g, norm_cross_g, norm_ffn_g,
              ab_w_in, lru_conv_w, lru_conv_b, lru_w_a, lru_b_a, lru_w_x, lru_b_x, lru_lambda,
              ssd_conv_w, ssd_conv_b, ssd_dt_bias, ssd_a_log, ssd_d, ssd_norm_g, ab_w_out,
              gla_w_in, gla_w_gate_up, gla_b_gate, gla_norm_g, gla_w_out,
              mem_norm_g, xa_w_q, xa_w_kv, xa_w_o,
              ffn_w_in, ffn_conv_w, ffn_conv_b, ffn_w_out, final_norm_g):
    memn = rms_norm(mem, mem_norm_g)
    for layer in range(DEPTH):
        j = layer // 2
        xn = rms_norm(x, norm_mix_g[layer])
        if layer % 2 == 0:
            x = x + ab_mixer(xn, ab_w_in[j], lru_conv_w[j], lru_conv_b[j], lru_w_a[j], lru_b_a[j],
                             lru_w_x[j], lru_b_x[j], lru_lambda[j], ssd_conv_w[j], ssd_conv_b[j],
                             ssd_dt_bias[j], ssd_a_log[j], ssd_d[j], ssd_norm_g[j], ab_w_out[j])
        else:
            x = x + gla_mixer(xn, gla_w_in[j], gla_w_gate_up[j], gla_b_gate[j], gla_norm_g[j], gla_w_out[j])
        x = x + cross_attention(rms_norm(x, norm_cross_g[layer]), memn,
                                xa_w_q[layer], xa_w_kv[layer], xa_w_o[layer])
        x = x + conv_ffn(rms_norm(x, norm_ffn_g[layer]), ffn_w_in[layer], ffn_conv_w[layer],
                         ffn_conv_b[layer], ffn_w_out[layer])
    return rms_norm(x, final_norm_g)
```

```python
import functools
import math

import jax
import jax.numpy as jnp
from jax import lax
from jax.experimental import pallas as pl
from jax.experimental.pallas import tpu as pltpu

F32 = jnp.float32
BF16 = jnp.bfloat16

EPS = 1e-6
CHUNK = 64
SUB = 16
LRU_C = 8.0
GATE_TAU = 16.0
XA_HEADS = 4
LANE = 128
SUBLANE = 8
VMEM_LIMIT = 56 * 1024 * 1024


def _cparams(*sem):
    return pltpu.CompilerParams(dimension_semantics=sem, vmem_limit_bytes=VMEM_LIMIT)


def _dot(a, b):
    return jnp.dot(a, b, preferred_element_type=F32)


def _dot_nt(a, b):
    return lax.dot_general(a, b, (((1,), (1,)), ((), ())), preferred_element_type=F32)


def _dot_tn(a, b):
    return lax.dot_general(a, b, (((0,), (0,)), ((), ())), preferred_element_type=F32)


def _split3(x):
    hi = x.astype(BF16)
    r1 = x - hi.astype(F32)
    mid = r1.astype(BF16)
    lo = (r1 - mid.astype(F32)).astype(BF16)
    return hi, mid, lo


def _sel_dot_left(sel, x):
    hi, mid, lo = _split3(x)
    return _dot(sel, hi) + _dot(sel, mid) + _dot(sel, lo)


def _sel_dot_right(x, sel):
    hi, mid, lo = _split3(x)
    return _dot(hi, sel) + _dot(mid, sel) + _dot(lo, sel)


def _tri(n):
    i = lax.broadcasted_iota(jnp.int32, (n, n), 0)
    j = lax.broadcasted_iota(jnp.int32, (n, n), 1)
    return jnp.where(j <= i, 1.0, 0.0).astype(BF16)


def _rms_rows(x, g):
    ms = jnp.mean(x * x, axis=-1, keepdims=True)
    return x * lax.rsqrt(ms + EPS) * g


def _conv_tile(in_ref, buf, cw, cb, rows):
    k = cw.shape[0]
    x = in_ref[...]
    buf[pl.ds(SUBLANE, rows), :] = x
    y = cb + cw[k - 1:k, :] * x
    for s in range(1, k):
        y = y + cw[k - 1 - s:k - s, :] * buf[pl.ds(SUBLANE - s, rows), :]
    buf[pl.ds(0, SUBLANE), :] = buf[pl.ds(rows, SUBLANE), :]
    return y


def _norm_matmul_body(x_ref, g_ref, w_ref, o_ref, xn_ref, *, rows):
    tm = x_ref.shape[0]

    @pl.when(pl.program_id(1) == 0)
    def _():
        def blk(r, c):
            sl = pl.ds(pl.multiple_of(r * rows, rows), rows)
            xn_ref[sl, :] = _rms_rows(x_ref[sl, :], g_ref[...]).astype(BF16)
            return c
        lax.fori_loop(0, tm // rows, blk, 0)

    o_ref[...] = _dot(xn_ref[...], w_ref[...]).astype(o_ref.dtype)


def _norm_matmul(x, g, w, *, tm, tn, out_dtype):
    m, d = x.shape
    n = w.shape[1]
    return pl.pallas_call(
        functools.partial(_norm_matmul_body, rows=min(tm, 256)),
        grid=(m // tm, n // tn),
        in_specs=[pl.BlockSpec((tm, d), lambda i, j: (i, 0)),
                  pl.BlockSpec((1, d), lambda i, j: (0, 0)),
                  pl.BlockSpec((d, tn), lambda i, j: (0, j))],
        out_specs=pl.BlockSpec((tm, tn), lambda i, j: (i, j)),
        out_shape=jax.ShapeDtypeStruct((m, n), out_dtype),
        scratch_shapes=[pltpu.VMEM((tm, d), BF16)],
        compiler_params=_cparams("parallel", "arbitrary"),
        name="norm_matmul",
    )(x, g.reshape(1, d), w)


def _matmul_res_body(*refs, n_lhs):
    a_refs = refs[:n_lhs]
    w_refs = refs[n_lhs:2 * n_lhs]
    res_ref = refs[2 * n_lhs]
    o_ref = refs[2 * n_lhs + 1]
    acc = res_ref[...]
    for a_ref, w_ref in zip(a_refs, w_refs):
        acc = acc + _dot(a_ref[...], w_ref[...])
    o_ref[...] = acc


def _matmul_res(lhs, ws, res, *, tm, tn):
    m, n = res.shape
    n_lhs = len(lhs)
    in_specs = [pl.BlockSpec((tm, a.shape[1]), lambda i, j: (i, 0)) for a in lhs]
    in_specs += [pl.BlockSpec((w.shape[0], tn), lambda i, j: (0, j)) for w in ws]
    in_specs += [pl.BlockSpec((tm, tn), lambda i, j: (i, j))]
    return pl.pallas_call(
        functools.partial(_matmul_res_body, n_lhs=n_lhs),
        grid=(m // tm, n // tn),
        in_specs=in_specs,
        out_specs=pl.BlockSpec((tm, tn), lambda i, j: (i, j)),
        out_shape=jax.ShapeDtypeStruct((m, n), F32),
        compiler_params=_cparams("parallel", "arbitrary"),
        name="matmul_res",
    )(*lhs, *ws, res)


def _rglru_body(gate_ref, xa_ref, cw_ref, cb_ref, wa_ref, ba_ref, wx_ref, bx_ref, lam_ref,
                o_ref, xbuf, a_s, u_s, hc):
    rows, tc = xa_ref.shape

    @pl.when(pl.program_id(2) == 0)
    def _():
        xbuf[pl.ds(0, SUBLANE), :] = jnp.zeros((SUBLANE, tc), F32)
        hc[...] = jnp.zeros((SUBLANE, tc), F32)

    xc = _conv_tile(xa_ref, xbuf, cw_ref[...], cb_ref[...], rows)
    sp = jax.nn.softplus(-lam_ref[...])
    for h in range(tc // LANE):
        sl = slice(LANE * h, LANE * (h + 1))
        xh = xc[:, sl]
        xb = xh.astype(BF16)
        r = jax.nn.sigmoid(_dot(xb, wa_ref[h]) + ba_ref[:, sl])
        i = jax.nn.sigmoid(_dot(xb, wx_ref[h]) + bx_ref[:, sl])
        log_a = -LRU_C * r * sp[:, sl]
        th = jnp.tanh(log_a)
        a_s[:, sl] = jnp.exp(log_a)
        u_s[:, sl] = jnp.sqrt(-2.0 * th / (1.0 - th)) * (i * xh)

    row = lax.broadcasted_iota(jnp.int32, (SUBLANE, tc), 0)

    def step(g, h_prev):
        sl = pl.ds(pl.multiple_of(g * SUBLANE, SUBLANE), SUBLANE)
        a = a_s[sl, :]
        u = u_s[sl, :]
        for s in (1, 2, 4):
            keep = row >= s
            a_sh = jnp.where(keep, pltpu.roll(a, s, 0), 1.0)
            u_sh = jnp.where(keep, pltpu.roll(u, s, 0), 0.0)
            u = a * u_sh + u
            a = a * a_sh
        h = a * h_prev + u
        u_s[sl, :] = h
        return jnp.broadcast_to(h[SUBLANE - 1:SUBLANE, :], (SUBLANE, tc))

    hc[...] = lax.fori_loop(0, rows // SUBLANE, step, hc[...])
    o_ref[...] = (jax.nn.gelu(gate_ref[...]) * u_s[...]).astype(o_ref.dtype)


def _rglru(proj, cw, cb, wa, ba, wx, bx, lam, *, bsz, seq, width, gate_col, x_col, rows, tc):
    n_t = seq // rows
    n_c = width // tc
    hpt = tc // LANE
    row_map = lambda b, c, t: b * n_t + t
    vec = lambda: pl.BlockSpec((1, tc), lambda b, c, t: (0, c))
    return pl.pallas_call(
        _rglru_body,
        grid=(bsz, n_c, n_t),
        in_specs=[pl.BlockSpec((rows, tc), lambda b, c, t: (row_map(b, c, t), gate_col // tc + c)),
                  pl.BlockSpec((rows, tc), lambda b, c, t: (row_map(b, c, t), x_col // tc + c)),
                  pl.BlockSpec((cw.shape[0], tc), lambda b, c, t: (0, c)),
                  vec(),
                  pl.BlockSpec((hpt, LANE, LANE), lambda b, c, t: (c, 0, 0)),
                  vec(),
                  pl.BlockSpec((hpt, LANE, LANE), lambda b, c, t: (c, 0, 0)),
                  vec(), vec()],
        out_specs=pl.BlockSpec((rows, tc), lambda b, c, t: (row_map(b, c, t), c)),
        out_shape=jax.ShapeDtypeStruct((bsz * seq, width), BF16),
        scratch_shapes=[pltpu.VMEM((rows + SUBLANE, tc), F32),
                        pltpu.VMEM((rows, tc), F32),
                        pltpu.VMEM((rows, tc), F32),
                        pltpu.VMEM((SUBLANE, tc), F32)],
        compiler_params=_cparams("parallel", "parallel", "arbitrary"),
        name="rglru",
    )(proj, proj, cw, cb.reshape(1, -1), wa, ba.reshape(1, -1), wx, bx.reshape(1, -1), lam.reshape(1, -1))


def _ssd_body(xs_ref, bm_ref, cm_ref, dt_ref, z_ref, cwx_ref, cwb_ref, cwc_ref, cbx_ref, cbb_ref, cbc_ref,
              dtb_ref, alog_ref, d_ref, ng_ref, e_ref, o_ref, bufx, bufb, bufc, state, *, heads, hdim):
    rows, wg = xs_ref.shape
    nst = bm_ref.shape[1]

    @pl.when(pl.program_id(2) == 0)
    def _():
        bufx[pl.ds(0, SUBLANE), :] = jnp.zeros((SUBLANE, wg), F32)
        bufb[pl.ds(0, SUBLANE), :] = jnp.zeros((SUBLANE, nst), F32)
        bufc[pl.ds(0, SUBLANE), :] = jnp.zeros((SUBLANE, nst), F32)
        state[...] = jnp.zeros(state.shape, F32)

    xs_all = jax.nn.silu(_conv_tile(xs_ref, bufx, cwx_ref[...], cbx_ref[...], rows))
    bm_all = jax.nn.silu(_conv_tile(bm_ref, bufb, cwb_ref[...], cbb_ref[...], rows)).astype(BF16)
    cm_all = jax.nn.silu(_conv_tile(cm_ref, bufc, cwc_ref[...], cbc_ref[...], rows)).astype(BF16)
    dt_all = jax.nn.softplus(dt_ref[...] + dtb_ref[...])
    dta_all = dt_all * (-jnp.exp(alog_ref[...]))
    expand = e_ref[...]
    dte_all = _sel_dot_right(dt_all, expand)
    tri = _tri(CHUNK)
    ii = lax.broadcasted_iota(jnp.int32, (CHUNK, CHUNK), 0)
    jj = lax.broadcasted_iota(jnp.int32, (CHUNK, CHUNK), 1)
    causal = jj <= ii

    for c in range(rows // CHUNK):
        sl = slice(CHUNK * c, CHUNK * (c + 1))
        xs = xs_all[sl]
        bm = bm_all[sl]
        cm = cm_all[sl]
        cs = _sel_dot_left(tri, dta_all[sl])
        cs_t = cs.T
        cse = _sel_dot_right(cs, expand)
        cs_last = cse[CHUNK - 1:CHUNK, :]
        xdt = xs * dte_all[sl]
        cb = _dot_nt(cm, bm)
        parts = []
        for r in range(heads):
            seg = cs[:, r:r + 1] - cs_t[r:r + 1, :]
            lmat = jnp.exp(jnp.where(causal, seg, -jnp.inf))
            parts.append(_dot((cb * lmat).astype(BF16), xdt[:, r * hdim:(r + 1) * hdim].astype(BF16)))
        y = jnp.concatenate(parts, axis=1)
        st = state[...]
        y = y + _dot(cm, st.astype(BF16)) * jnp.exp(cse)
        state[...] = st * jnp.exp(cs_last) + _dot_tn(bm, (xdt * jnp.exp(cs_last - cse)).astype(BF16))
        y = y + d_ref[...] * xs
        y = y * jax.nn.silu(z_ref[sl, :])
        y = y * lax.rsqrt(jnp.mean(y * y, axis=-1, keepdims=True) + EPS) * ng_ref[...]
        o_ref[sl, :] = y.astype(o_ref.dtype)


def _ssd(proj, cw, cb, dtb, alog, dexp, ng, expand, *, bsz, seq, groups, heads, hdim, nst,
         z_col, xs_col, bm_col, cm_col, dt_col, rows):
    n_t = seq // rows
    wg = heads * hdim
    width = groups * wg
    row_map = lambda b, g, t: b * n_t + t
    k = cw.shape[0]
    cb2 = cb.reshape(1, -1)

    def col(width_, base):
        return pl.BlockSpec((rows, width_), lambda b, g, t: (row_map(b, g, t), base // width_ + g))

    def par(nrow, width_, base):
        return pl.BlockSpec((nrow, width_), lambda b, g, t: (0, base // width_ + g))

    return pl.pallas_call(
        functools.partial(_ssd_body, heads=heads, hdim=hdim),
        grid=(bsz, groups, n_t),
        in_specs=[col(wg, xs_col), col(nst, bm_col), col(nst, cm_col), col(LANE, dt_col), col(wg, z_col),
                  par(k, wg, 0), par(k, nst, width), par(k, nst, width + groups * nst),
                  par(1, wg, 0), par(1, nst, width), par(1, nst, width + groups * nst),
                  par(1, LANE, 0), par(1, LANE, 0), par(1, wg, 0), par(1, wg, 0),
                  pl.BlockSpec((LANE, wg), lambda b, g, t: (0, 0))],
        out_specs=pl.BlockSpec((rows, wg), lambda b, g, t: (row_map(b, g, t), g)),
        out_shape=jax.ShapeDtypeStruct((bsz * seq, width), BF16),
        scratch_shapes=[pltpu.VMEM((rows + SUBLANE, wg), F32),
                        pltpu.VMEM((rows + SUBLANE, nst), F32),
                        pltpu.VMEM((rows + SUBLANE, nst), F32),
                        pltpu.VMEM((nst, wg), F32)],
        compiler_params=_cparams("parallel", "parallel", "arbitrary"),
        name="ssd",
    )(proj, proj, proj, proj, proj, cw, cw, cw, cb2, cb2, cb2, dtb, alog, dexp, ng, expand)


def _gla_body(q_ref, k_ref, v_ref, r_ref, gl_ref, wg_ref, bg_ref, ng_ref, o_ref, state, *, q_scale):
    rows, dk = q_ref.shape

    @pl.when(pl.program_id(2) == 0)
    def _():
        state[...] = jnp.zeros(state.shape, F32)

    la_all = jax.nn.log_sigmoid(_dot(gl_ref[...].astype(BF16), wg_ref[...]) + bg_ref[...]) / GATE_TAU
    tri = _tri(CHUNK)
    lane = lax.broadcasted_iota(jnp.int32, (SUB, CHUNK), 1)
    srow = lax.broadcasted_iota(jnp.int32, (SUB, CHUNK), 0)

    for c in range(rows // CHUNK):
        sl = slice(CHUNK * c, CHUNK * (c + 1))
        q = q_ref[sl, :] * q_scale
        k = k_ref[sl, :]
        v = v_ref[sl, :].astype(BF16)
        b = _sel_dot_left(tri, la_all[sl])
        b_last = b[CHUNK - 1:CHUNK, :]
        st = state[...]
        o = _dot_nt((q * jnp.exp(b)).astype(BF16), st.astype(BF16))
        state[...] = st * jnp.exp(b_last) + _dot_tn(v, (k * jnp.exp(b_last - b)).astype(BF16))

        att_rows = []
        for blk in range(CHUNK // SUB):
            bs = slice(SUB * blk, SUB * (blk + 1))
            qi, ki, bi = q[bs], k[bs], b[bs]
            att = jnp.zeros((SUB, CHUNK), F32)
            for j in range(SUB):
                e = jnp.exp(jnp.minimum(bi - bi[j:j + 1, :], 0.0))
                s = jnp.sum(qi * ki[j:j + 1, :] * e, axis=-1, keepdims=True)
                att = jnp.where(lane == SUB * blk + j, s, att)
            att = jnp.where(lane - SUB * blk <= srow, att, 0.0)
            if blk > 0:
                bref = b[SUB * blk - 1:SUB * blk, :]
                qt = (qi * jnp.exp(bi - bref)).astype(BF16)
                kt = (k * jnp.exp(jnp.minimum(bref - b, 0.0))).astype(BF16)
                att = jnp.where(lane < SUB * blk, _dot_nt(qt, kt), att)
            att_rows.append(att)
        att = jnp.concatenate(att_rows, axis=0)
        o = o + _dot(att.astype(BF16), v)
        o = o * lax.rsqrt(jnp.mean(o * o, axis=-1, keepdims=True) + EPS)
        o = o * ng_ref[...] * jax.nn.silu(r_ref[sl, :])
        o_ref[sl, :] = o.astype(o_ref.dtype)


def _gla(proj, wg, bg, ng, *, bsz, seq, heads, dk, dv, q_col, k_col, v_col, r_col, gl_col, rows):
    n_t = seq // rows
    row_map = lambda b, h, t: b * n_t + t

    def col(width_, base):
        return pl.BlockSpec((rows, width_), lambda b, h, t: (row_map(b, h, t), base // width_ + h))

    return pl.pallas_call(
        functools.partial(_gla_body, q_scale=dk ** -0.5),
        grid=(bsz, heads, n_t),
        in_specs=[col(dk, q_col), col(dk, k_col), col(dv, v_col), col(dv, r_col),
                  pl.BlockSpec((rows, LANE), lambda b, h, t: (row_map(b, h, t), gl_col // LANE)),
                  pl.BlockSpec((LANE, dk), lambda b, h, t: (0, h)),
                  pl.BlockSpec((1, dk), lambda b, h, t: (0, h)),
                  pl.BlockSpec((1, dv), lambda b, h, t: (0, h))],
        out_specs=pl.BlockSpec((rows, dv), lambda b, h, t: (row_map(b, h, t), h)),
        out_shape=jax.ShapeDtypeStruct((bsz * seq, heads * dv), BF16),
        scratch_shapes=[pltpu.VMEM((dv, dk), F32)],
        compiler_params=_cparams("parallel", "parallel", "arbitrary"),
        name="gla",
    )(proj, proj, proj, proj, proj, wg, bg, ng)


def _xattn_body(q_ref, k_ref, v_ref, o_ref, *, heads, scale):
    dh = q_ref.shape[1] // heads
    for h in range(heads):
        sl = slice(dh * h, dh * (h + 1))
        s = _dot_nt(q_ref[:, sl], k_ref[:, sl]) * scale
        s = s - jnp.max(s, axis=-1, keepdims=True)
        p = jnp.exp(s)
        p = p / jnp.sum(p, axis=-1, keepdims=True)
        o_ref[:, sl] = _dot(p.astype(BF16), v_ref[:, sl]).astype(o_ref.dtype)


def _xattn(q, kv, *, seq, mem_len, tm):
    m, d = q.shape
    per_b = seq // tm
    return pl.pallas_call(
        functools.partial(_xattn_body, heads=XA_HEADS, scale=(d // XA_HEADS) ** -0.5),
        grid=(m // tm,),
        in_specs=[pl.BlockSpec((tm, d), lambda i: (i, 0)),
                  pl.BlockSpec((mem_len, d), lambda i: (i // per_b, 0)),
                  pl.BlockSpec((mem_len, d), lambda i: (i // per_b, 1))],
        out_specs=pl.BlockSpec((tm, d), lambda i: (i, 0)),
        out_shape=jax.ShapeDtypeStruct((m, d), BF16),
        compiler_params=_cparams("parallel"),
        name="xattn",
    )(q, kv, kv)


def _ffn_body(x_ref, g_ref, wv_ref, wg_ref, cwv_ref, cwg_ref, cbv_ref, cbg_ref, wo_ref, fg_ref,
              o_ref, xn_ref, bufv, bufg, carry, *, rows, tiles_per_seq, final_norm):
    tm = x_ref.shape[0]
    i = pl.program_id(0)
    j = pl.program_id(1)
    n_j = pl.num_programs(1)

    @pl.when(j == 0)
    def _():
        def blk(r, c):
            sl = pl.ds(pl.multiple_of(r * rows, rows), rows)
            xn_ref[sl, :] = _rms_rows(x_ref[sl, :], g_ref[...]).astype(BF16)
            return c
        lax.fori_loop(0, tm // rows, blk, 0)
        o_ref[...] = x_ref[...]

    @pl.when(i % tiles_per_seq == 0)
    def _():
        carry[j] = jnp.zeros(carry.shape[1:], F32)

    xn = xn_ref[...]
    bufv[pl.ds(0, SUBLANE), :] = carry[j, pl.ds(0, SUBLANE), :]
    bufg[pl.ds(0, SUBLANE), :] = carry[j, pl.ds(SUBLANE, SUBLANE), :]

    def conv(w_ref, buf, cw, cb):
        h = _dot(xn, w_ref[...])
        buf[pl.ds(SUBLANE, tm), :] = h
        k = cw.shape[0]
        y = cb + cw[k - 1:k, :] * h
        for s in range(1, k):
            y = y + cw[k - 1 - s:k - s, :] * buf[pl.ds(SUBLANE - s, tm), :]
        return y

    val = conv(wv_ref, bufv, cwv_ref[...], cbv_ref[...])
    gate = conv(wg_ref, bufg, cwg_ref[...], cbg_ref[...])
    carry[j, pl.ds(0, SUBLANE), :] = bufv[pl.ds(tm, SUBLANE), :]
    carry[j, pl.ds(SUBLANE, SUBLANE), :] = bufg[pl.ds(tm, SUBLANE), :]
    act = (val * jax.nn.gelu(gate)).astype(BF16)
    o_ref[...] += _dot(act, wo_ref[...])

    if final_norm:
        @pl.when(j == n_j - 1)
        def _():
            def blk(r, c):
                sl = pl.ds(pl.multiple_of(r * rows, rows), rows)
                o_ref[sl, :] = _rms_rows(o_ref[sl, :], fg_ref[...])
                return c
            lax.fori_loop(0, tm // rows, blk, 0)


def _ffn(x, g, w_in, cw, cb, w_out, fg, *, seq, tm, tf, final_norm):
    m, d = x.shape
    d_ff = w_out.shape[0]
    n_j = d_ff // tf
    k = cw.shape[0]
    cb2 = cb.reshape(1, -1)
    return pl.pallas_call(
        functools.partial(_ffn_body, rows=min(tm, 256), tiles_per_seq=seq // tm, final_norm=final_norm),
        grid=(m // tm, n_j),
        in_specs=[pl.BlockSpec((tm, d), lambda i, j: (i, 0)),
                  pl.BlockSpec((1, d), lambda i, j: (0, 0)),
                  pl.BlockSpec((d, tf), lambda i, j: (0, j)),
                  pl.BlockSpec((d, tf), lambda i, j: (0, n_j + j)),
                  pl.BlockSpec((k, tf), lambda i, j: (0, j)),
                  pl.BlockSpec((k, tf), lambda i, j: (0, n_j + j)),
                  pl.BlockSpec((1, tf), lambda i, j: (0, j)),
                  pl.BlockSpec((1, tf), lambda i, j: (0, n_j + j)),
                  pl.BlockSpec((tf, d), lambda i, j: (j, 0)),
                  pl.BlockSpec((1, d), lambda i, j: (0, 0))],
        out_specs=pl.BlockSpec((tm, d), lambda i, j: (i, 0)),
        out_shape=jax.ShapeDtypeStruct((m, d), F32),
        scratch_shapes=[pltpu.VMEM((tm, d), BF16),
                        pltpu.VMEM((tm + SUBLANE, tf), F32),
                        pltpu.VMEM((tm + SUBLANE, tf), F32),
                        pltpu.VMEM((n_j, 2 * SUBLANE, tf), F32)],
        compiler_params=_cparams("arbitrary", "arbitrary"),
        name="conv_ffn",
    )(x, g.reshape(1, d), w_in, w_in, cw, cw, cb2, cb2, w_out, fg.reshape(1, d))


def _pad_cols(w, n):
    return jnp.pad(w, ((0, 0), (0, n - w.shape[1])))


def _group_lanes(v, groups, heads):
    lead = v.shape[:-1]
    v = v.reshape(lead + (groups, heads))
    v = jnp.pad(v, [(0, 0)] * len(lead) + [(0, 0), (0, LANE - heads)])
    return v.reshape(lead + (groups * LANE,))


def kernel(x, mem, norm_mix_g, norm_cross_g, norm_ffn_g, ab_w_in, lru_conv_w, lru_conv_b, lru_w_a, lru_b_a, lru_w_x, lru_b_x, lru_lambda, ssd_conv_w, ssd_conv_b, ssd_dt_bias, ssd_a_log, ssd_d, ssd_norm_g, ab_w_out, gla_w_in, gla_w_gate_up, gla_b_gate, gla_norm_g, gla_w_out, mem_norm_g, xa_w_q, xa_w_kv, xa_w_o, ffn_w_in, ffn_conv_w, ffn_conv_b, ffn_w_out, final_norm_g):
    bsz, seq, d = x.shape
    mem_len = mem.shape[1]
    depth = norm_mix_g.shape[0]
    m = bsz * seq
    assert seq % 512 == 0 and d % 512 == 0

    w_a = lru_lambda.shape[1]
    h_b = ssd_dt_bias.shape[1]
    w_b = ssd_norm_g.shape[1]
    p_b = w_b // h_b
    conv_dim = ssd_conv_w.shape[2]
    g_b = 4
    n_b = (conv_dim - w_b) // (2 * g_b)
    r_b = h_b // g_b
    assert ab_w_in.shape[2] == 2 * w_a + w_b + conv_dim + h_b and r_b * p_b == 512 and n_b == LANE
    dk_c = gla_b_gate.shape[1]
    dv_c = gla_norm_g.shape[1]
    rank = gla_w_gate_up.shape[1]
    h_c = 4
    assert gla_w_in.shape[2] == 2 * dk_c + 2 * dv_c + rank and rank <= LANE

    x2 = x.reshape(m, d)
    kv_in = mem.reshape(bsz * mem_len, d)
    expand = (lax.broadcasted_iota(jnp.int32, (LANE, r_b * p_b), 1) // p_b
              == lax.broadcasted_iota(jnp.int32, (LANE, r_b * p_b), 0)).astype(BF16)

    for layer in range(depth):
        j = layer // 2
        if layer % 2 == 0:
            main = 2 * w_a + w_b + conv_dim
            w_in = jnp.concatenate([ab_w_in[j][:, :main], _group_lanes(ab_w_in[j][:, main:], g_b, r_b)], axis=1).astype(BF16)
            proj = _norm_matmul(x2, norm_mix_g[layer], w_in, tm=1024, tn=512, out_dtype=F32)
            y_a = _rglru(proj, lru_conv_w[j], lru_conv_b[j], lru_w_a[j].astype(BF16), lru_b_a[j],
                         lru_w_x[j].astype(BF16), lru_b_x[j], lru_lambda[j],
                         bsz=bsz, seq=seq, width=w_a, gate_col=0, x_col=w_a, rows=256, tc=512)
            y_b = _ssd(proj, ssd_conv_w[j], ssd_conv_b[j],
                       _group_lanes(ssd_dt_bias[j], g_b, r_b).reshape(1, -1),
                       _group_lanes(ssd_a_log[j], g_b, r_b).reshape(1, -1),
                       jnp.repeat(ssd_d[j], p_b).reshape(1, -1), ssd_norm_g[j].reshape(1, -1), expand,
                       bsz=bsz, seq=seq, groups=g_b, heads=r_b, hdim=p_b, nst=n_b,
                       z_col=2 * w_a, xs_col=2 * w_a + w_b, bm_col=2 * w_a + 2 * w_b,
                       cm_col=2 * w_a + 2 * w_b + g_b * n_b, dt_col=main, rows=256)
            w_out = ab_w_out[j].astype(BF16)
            x2 = _matmul_res([y_a, y_b], [w_out[:w_a], w_out[w_a:]], x2, tm=512, tn=1024)
        else:
            main = 2 * dk_c + 2 * dv_c
            w_in = _pad_cols(gla_w_in[j], main + LANE).astype(BF16)
            proj = _norm_matmul(x2, norm_mix_g[layer], w_in, tm=1024, tn=896, out_dtype=F32)
            wg = jnp.pad(gla_w_gate_up[j], ((0, LANE - rank), (0, 0))).astype(BF16)
            o = _gla(proj, wg, gla_b_gate[j].reshape(1, -1), gla_norm_g[j].reshape(1, -1),
                     bsz=bsz, seq=seq, heads=h_c, dk=dk_c // h_c, dv=dv_c // h_c,
                     q_col=0, k_col=dk_c, v_col=2 * dk_c, r_col=2 * dk_c + dv_c, gl_col=main, rows=256)
            x2 = _matmul_res([o], [gla_w_out[j].astype(BF16)], x2, tm=512, tn=1024)

        q = _norm_matmul(x2, norm_cross_g[layer], xa_w_q[layer].astype(BF16), tm=1024, tn=1024, out_dtype=BF16)
        kv = _norm_matmul(kv_in, mem_norm_g, xa_w_kv[layer].astype(BF16), tm=bsz * mem_len, tn=1024, out_dtype=BF16)
        att = _xattn(q, kv, seq=seq, mem_len=mem_len, tm=512)
        x2 = _matmul_res([att], [xa_w_o[layer].astype(BF16)], x2, tm=512, tn=1024)

        x2 = _ffn(x2, norm_ffn_g[layer], ffn_w_in[layer].astype(BF16), ffn_conv_w[layer], ffn_conv_b[layer],
                  ffn_w_out[layer].astype(BF16), final_norm_g, seq=seq, tm=512, tf=1024,
                  final_norm=(layer == depth - 1))
    return x2.reshape(bsz, seq, d)
```

```python
import functools
import math

import jax
import jax.numpy as jnp
from jax import lax
from jax.experimental import pallas as pl
from jax.experimental.pallas import tpu as pltpu

F32 = jnp.float32
BF16 = jnp.bfloat16

EPS = 1e-6
CHUNK = 64
SUB = 16
LRU_C = 8.0
GATE_TAU = 16.0
XA_HEADS = 4
LANE = 128
SUBLANE = 8
VMEM_LIMIT = 56 * 1024 * 1024


def _cparams(*sem):
    return pltpu.CompilerParams(dimension_semantics=sem, vmem_limit_bytes=VMEM_LIMIT)


def _dot(a, b):
    return jnp.dot(a, b, preferred_element_type=F32)


def _dot_nt(a, b):
    return lax.dot_general(a, b, (((1,), (1,)), ((), ())), preferred_element_type=F32)


def _dot_tn(a, b):
    return lax.dot_general(a, b, (((0,), (0,)), ((), ())), preferred_element_type=F32)


def _split3(x):
    hi = x.astype(BF16)
    r1 = x - hi.astype(F32)
    mid = r1.astype(BF16)
    lo = (r1 - mid.astype(F32)).astype(BF16)
    return hi, mid, lo


def _sel_dot_left(sel, x):
    hi, mid, lo = _split3(x)
    return _dot(sel, hi) + _dot(sel, mid) + _dot(sel, lo)


def _sel_dot_right(x, sel):
    hi, mid, lo = _split3(x)
    return _dot(hi, sel) + _dot(mid, sel) + _dot(lo, sel)


def _tri(n):
    i = lax.broadcasted_iota(jnp.int32, (n, n), 0)
    j = lax.broadcasted_iota(jnp.int32, (n, n), 1)
    return jnp.where(j <= i, 1.0, 0.0).astype(BF16)


def _sigmoid(x):
    return 0.5 * jnp.tanh(0.5 * x) + 0.5


def _silu(x):
    return x * _sigmoid(x)


def _rms_rows(x, g):
    ms = jnp.mean(x * x, axis=-1, keepdims=True)
    return x * lax.rsqrt(ms + EPS) * g


def _conv_tile(in_ref, buf, cw, cb, rows):
    k = cw.shape[0]
    x = in_ref[...]
    buf[pl.ds(SUBLANE, rows), :] = x
    y = cb + cw[k - 1:k, :] * x
    for s in range(1, k):
        y = y + cw[k - 1 - s:k - s, :] * buf[pl.ds(SUBLANE - s, rows), :]
    buf[pl.ds(0, SUBLANE), :] = buf[pl.ds(rows, SUBLANE), :]
    return y


def _norm_matmul_body(x_ref, g_ref, w_ref, o_ref, xn_ref, *, rows):
    tm = x_ref.shape[0]

    @pl.when(pl.program_id(1) == 0)
    def _():
        def blk(r, c):
            sl = pl.ds(pl.multiple_of(r * rows, rows), rows)
            xn_ref[sl, :] = _rms_rows(x_ref[sl, :], g_ref[...]).astype(BF16)
            return c
        lax.fori_loop(0, tm // rows, blk, 0)

    o_ref[...] = _dot(xn_ref[...], w_ref[...]).astype(o_ref.dtype)


def _norm_matmul(x, g, w, *, tm, tn, out_dtype):
    m, d = x.shape
    n = w.shape[1]
    return pl.pallas_call(
        functools.partial(_norm_matmul_body, rows=min(tm, 256)),
        grid=(m // tm, n // tn),
        in_specs=[pl.BlockSpec((tm, d), lambda i, j: (i, 0)),
                  pl.BlockSpec((1, d), lambda i, j: (0, 0)),
                  pl.BlockSpec((d, tn), lambda i, j: (0, j))],
        out_specs=pl.BlockSpec((tm, tn), lambda i, j: (i, j)),
        out_shape=jax.ShapeDtypeStruct((m, n), out_dtype),
        scratch_shapes=[pltpu.VMEM((tm, d), BF16)],
        compiler_params=_cparams("parallel", "arbitrary"),
        name="norm_matmul",
    )(x, g.reshape(1, d), w)


def _matmul_res_body(*refs, n_lhs):
    a_refs = refs[:n_lhs]
    w_refs = refs[n_lhs:2 * n_lhs]
    res_ref = refs[2 * n_lhs]
    o_ref = refs[2 * n_lhs + 1]
    acc = res_ref[...]
    for a_ref, w_ref in zip(a_refs, w_refs):
        acc = acc + _dot(a_ref[...], w_ref[...])
    o_ref[...] = acc


def _matmul_res(lhs, ws, res, *, tm, tn):
    m, n = res.shape
    n_lhs = len(lhs)
    in_specs = [pl.BlockSpec((tm, a.shape[1]), lambda j, i: (i, 0)) for a in lhs]
    in_specs += [pl.BlockSpec((w.shape[0], tn), lambda j, i: (0, j)) for w in ws]
    in_specs += [pl.BlockSpec((tm, tn), lambda j, i: (i, j))]
    return pl.pallas_call(
        functools.partial(_matmul_res_body, n_lhs=n_lhs),
        grid=(n // tn, m // tm),
        in_specs=in_specs,
        out_specs=pl.BlockSpec((tm, tn), lambda j, i: (i, j)),
        out_shape=jax.ShapeDtypeStruct((m, n), F32),
        compiler_params=_cparams("parallel", "arbitrary"),
        name="matmul_res",
    )(*lhs, *ws, res)


def _rglru_body(gate_ref, xa_ref, cw_ref, cb_ref, wa_ref, ba_ref, wx_ref, bx_ref, lam_ref,
                o_ref, xbuf, a_s, u_s, hc):
    rows, tc = xa_ref.shape

    @pl.when(pl.program_id(2) == 0)
    def _():
        xbuf[pl.ds(0, SUBLANE), :] = jnp.zeros((SUBLANE, tc), F32)
        hc[...] = jnp.zeros((SUBLANE, tc), F32)

    xc = _conv_tile(xa_ref, xbuf, cw_ref[...], cb_ref[...], rows)
    sp = jax.nn.softplus(-lam_ref[...])
    for h in range(tc // LANE):
        sl = slice(LANE * h, LANE * (h + 1))
        xh = xc[:, sl]
        xb = xh.astype(BF16)
        r = _sigmoid(_dot(xb, wa_ref[h]) + ba_ref[:, sl])
        i = _sigmoid(_dot(xb, wx_ref[h]) + bx_ref[:, sl])
        log_a = -LRU_C * r * sp[:, sl]
        th = jnp.tanh(log_a)
        a_s[:, sl] = jnp.exp(log_a)
        u_s[:, sl] = jnp.sqrt(-2.0 * th / (1.0 - th)) * (i * xh)

    row = lax.broadcasted_iota(jnp.int32, (SUBLANE, tc), 0)

    def step(g, h_prev):
        sl = pl.ds(pl.multiple_of(g * SUBLANE, SUBLANE), SUBLANE)
        a = a_s[sl, :]
        u = u_s[sl, :]
        for s in (1, 2, 4):
            keep = row >= s
            a_sh = jnp.where(keep, pltpu.roll(a, s, 0), 1.0)
            u_sh = jnp.where(keep, pltpu.roll(u, s, 0), 0.0)
            u = a * u_sh + u
            a = a * a_sh
        h = a * h_prev + u
        u_s[sl, :] = h
        return jnp.broadcast_to(h[SUBLANE - 1:SUBLANE, :], (SUBLANE, tc))

    hc[...] = lax.fori_loop(0, rows // SUBLANE, step, hc[...])
    o_ref[...] = (jax.nn.gelu(gate_ref[...]) * u_s[...]).astype(o_ref.dtype)


def _rglru(proj, cw, cb, wa, ba, wx, bx, lam, *, bsz, seq, width, gate_col, x_col, rows, tc):
    n_t = seq // rows
    n_c = width // tc
    hpt = tc // LANE
    row_map = lambda b, c, t: b * n_t + t
    vec = lambda: pl.BlockSpec((1, tc), lambda b, c, t: (0, c))
    return pl.pallas_call(
        _rglru_body,
        grid=(bsz, n_c, n_t),
        in_specs=[pl.BlockSpec((rows, tc), lambda b, c, t: (row_map(b, c, t), gate_col // tc + c)),
                  pl.BlockSpec((rows, tc), lambda b, c, t: (row_map(b, c, t), x_col // tc + c)),
                  pl.BlockSpec((cw.shape[0], tc), lambda b, c, t: (0, c)),
                  vec(),
                  pl.BlockSpec((hpt, LANE, LANE), lambda b, c, t: (c, 0, 0)),
                  vec(),
                  pl.BlockSpec((hpt, LANE, LANE), lambda b, c, t: (c, 0, 0)),
                  vec(), vec()],
        out_specs=pl.BlockSpec((rows, tc), lambda b, c, t: (row_map(b, c, t), c)),
        out_shape=jax.ShapeDtypeStruct((bsz * seq, width), BF16),
        scratch_shapes=[pltpu.VMEM((rows + SUBLANE, tc), F32),
                        pltpu.VMEM((rows, tc), F32),
                        pltpu.VMEM((rows, tc), F32),
                        pltpu.VMEM((SUBLANE, tc), F32)],
        compiler_params=_cparams("parallel", "parallel", "arbitrary"),
        name="rglru",
    )(proj, proj, cw, cb.reshape(1, -1), wa, ba.reshape(1, -1), wx, bx.reshape(1, -1), lam.reshape(1, -1))


def _ssd_body(xs_ref, bm_ref, cm_ref, dt_ref, z_ref, cwx_ref, cwb_ref, cwc_ref, cbx_ref, cbb_ref, cbc_ref,
              dtb_ref, alog_ref, d_ref, ng_ref, e_ref, o_ref, bufx, bufb, bufc, state, *, heads, hdim):
    rows, wg = xs_ref.shape
    nst = bm_ref.shape[1]

    @pl.when(pl.program_id(2) == 0)
    def _():
        bufx[pl.ds(0, SUBLANE), :] = jnp.zeros((SUBLANE, wg), F32)
        bufb[pl.ds(0, SUBLANE), :] = jnp.zeros((SUBLANE, nst), F32)
        bufc[pl.ds(0, SUBLANE), :] = jnp.zeros((SUBLANE, nst), F32)
        state[...] = jnp.zeros(state.shape, F32)

    xs_all = _silu(_conv_tile(xs_ref, bufx, cwx_ref[...], cbx_ref[...], rows))
    bm_all = _silu(_conv_tile(bm_ref, bufb, cwb_ref[...], cbb_ref[...], rows)).astype(BF16)
    cm_all = _silu(_conv_tile(cm_ref, bufc, cwc_ref[...], cbc_ref[...], rows)).astype(BF16)
    dt_all = jax.nn.softplus(dt_ref[...] + dtb_ref[...])
    dta_all = dt_all * (-jnp.exp(alog_ref[...]))
    expand = e_ref[...]
    dte_all = _sel_dot_right(dt_all, expand)
    tri = _tri(CHUNK)
    ii = lax.broadcasted_iota(jnp.int32, (CHUNK, CHUNK), 0)
    jj = lax.broadcasted_iota(jnp.int32, (CHUNK, CHUNK), 1)
    causal = jj <= ii

    for c in range(rows // CHUNK):
        sl = slice(CHUNK * c, CHUNK * (c + 1))
        xs = xs_all[sl]
        bm = bm_all[sl]
        cm = cm_all[sl]
        cs = _sel_dot_left(tri, dta_all[sl])
        cs_t = cs.T
        cse = _sel_dot_right(cs, expand)
        cs_last = cse[CHUNK - 1:CHUNK, :]
        xdt = xs * dte_all[sl]
        cb = _dot_nt(cm, bm)
        parts = []
        for r in range(heads):
            seg = cs[:, r:r + 1] - cs_t[r:r + 1, :]
            lmat = jnp.exp(jnp.where(causal, seg, -jnp.inf))
            parts.append(_dot((cb * lmat).astype(BF16), xdt[:, r * hdim:(r + 1) * hdim].astype(BF16)))
        y = jnp.concatenate(parts, axis=1)
        st = state[...]
        y = y + _dot(cm, st.astype(BF16)) * jnp.exp(cse)
        state[...] = st * jnp.exp(cs_last) + _dot_tn(bm, (xdt * jnp.exp(cs_last - cse)).astype(BF16))
        y = y + d_ref[...] * xs
        y = y * _silu(z_ref[sl, :])
        y = y * lax.rsqrt(jnp.mean(y * y, axis=-1, keepdims=True) + EPS) * ng_ref[...]
        o_ref[sl, :] = y.astype(o_ref.dtype)


def _ssd(proj, cw, cb, dtb, alog, dexp, ng, expand, *, bsz, seq, groups, heads, hdim, nst,
         z_col, xs_col, bm_col, cm_col, dt_col, rows):
    n_t = seq // rows
    wg = heads * hdim
    width = groups * wg
    row_map = lambda b, g, t: b * n_t + t
    k = cw.shape[0]
    cb2 = cb.reshape(1, -1)

    def col(width_, base):
        return pl.BlockSpec((rows, width_), lambda b, g, t: (row_map(b, g, t), base // width_ + g))

    def par(nrow, width_, base):
        return pl.BlockSpec((nrow, width_), lambda b, g, t: (0, base // width_ + g))

    return pl.pallas_call(
        functools.partial(_ssd_body, heads=heads, hdim=hdim),
        grid=(bsz, groups, n_t),
        in_specs=[col(wg, xs_col), col(nst, bm_col), col(nst, cm_col), col(LANE, dt_col), col(wg, z_col),
                  par(k, wg, 0), par(k, nst, width), par(k, nst, width + groups * nst),
                  par(1, wg, 0), par(1, nst, width), par(1, nst, width + groups * nst),
                  par(1, LANE, 0), par(1, LANE, 0), par(1, wg, 0), par(1, wg, 0),
                  pl.BlockSpec((LANE, wg), lambda b, g, t: (0, 0))],
        out_specs=pl.BlockSpec((rows, wg), lambda b, g, t: (row_map(b, g, t), g)),
        out_shape=jax.ShapeDtypeStruct((bsz * seq, width), BF16),
        scratch_shapes=[pltpu.VMEM((rows + SUBLANE, wg), F32),
                        pltpu.VMEM((rows + SUBLANE, nst), F32),
                        pltpu.VMEM((rows + SUBLANE, nst), F32),
                        pltpu.VMEM((nst, wg), F32)],
        compiler_params=_cparams("parallel", "parallel", "arbitrary"),
        name="ssd",
    )(proj, proj, proj, proj, proj, cw, cw, cw, cb2, cb2, cb2, dtb, alog, dexp, ng, expand)


def _gla_body(q_ref, k_ref, v_ref, r_ref, gl_ref, wg_ref, bg_ref, ng_ref, o_ref, state, *, q_scale):
    rows, dk = q_ref.shape

    @pl.when(pl.program_id(2) == 0)
    def _():
        state[...] = jnp.zeros(state.shape, F32)

    la_all = jax.nn.log_sigmoid(_dot(gl_ref[...].astype(BF16), wg_ref[...]) + bg_ref[...]) / GATE_TAU
    tri = _tri(CHUNK)
    lane = lax.broadcasted_iota(jnp.int32, (SUB, CHUNK), 1)
    srow = lax.broadcasted_iota(jnp.int32, (SUB, CHUNK), 0)

    for c in range(rows // CHUNK):
        sl = slice(CHUNK * c, CHUNK * (c + 1))
        q = q_ref[sl, :] * q_scale
        k = k_ref[sl, :]
        v = v_ref[sl, :].astype(BF16)
        b = _sel_dot_left(tri, la_all[sl])
        b_last = b[CHUNK - 1:CHUNK, :]
        st = state[...]
        o = _dot_nt((q * jnp.exp(b)).astype(BF16), st.astype(BF16))
        state[...] = st * jnp.exp(b_last) + _dot_tn(v, (k * jnp.exp(b_last - b)).astype(BF16))

        att_rows = []
        for blk in range(CHUNK // SUB):
            bs = slice(SUB * blk, SUB * (blk + 1))
            qi, ki, bi = q[bs], k[bs], b[bs]
            att = jnp.zeros((SUB, CHUNK), F32)
            for j in range(SUB):
                e = jnp.exp(jnp.minimum(bi - bi[j:j + 1, :], 0.0))
                s = jnp.sum(qi * ki[j:j + 1, :] * e, axis=-1, keepdims=True)
                att = jnp.where(lane == SUB * blk + j, s, att)
            att = jnp.where(lane - SUB * blk <= srow, att, 0.0)
            if blk > 0:
                bref = b[SUB * blk - 1:SUB * blk, :]
                qt = (qi * jnp.exp(bi - bref)).astype(BF16)
                kt = (k * jnp.exp(jnp.minimum(bref - b, 0.0))).astype(BF16)
                att = jnp.where(lane < SUB * blk, _dot_nt(qt, kt), att)
            att_rows.append(att)
        att = jnp.concatenate(att_rows, axis=0)
        o = o + _dot(att.astype(BF16), v)
        o = o * lax.rsqrt(jnp.mean(o * o, axis=-1, keepdims=True) + EPS)
        o = o * ng_ref[...] * _silu(r_ref[sl, :])
        o_ref[sl, :] = o.astype(o_ref.dtype)


def _gla(proj, wg, bg, ng, *, bsz, seq, heads, dk, dv, q_col, k_col, v_col, r_col, gl_col, rows):
    n_t = seq // rows
    row_map = lambda b, h, t: b * n_t + t

    def col(width_, base):
        return pl.BlockSpec((rows, width_), lambda b, h, t: (row_map(b, h, t), base // width_ + h))

    return pl.pallas_call(
        functools.partial(_gla_body, q_scale=dk ** -0.5),
        grid=(bsz, heads, n_t),
        in_specs=[col(dk, q_col), col(dk, k_col), col(dv, v_col), col(dv, r_col),
                  pl.BlockSpec((rows, LANE), lambda b, h, t: (row_map(b, h, t), gl_col // LANE)),
                  pl.BlockSpec((LANE, dk), lambda b, h, t: (0, h)),
                  pl.BlockSpec((1, dk), lambda b, h, t: (0, h)),
                  pl.BlockSpec((1, dv), lambda b, h, t: (0, h))],
        out_specs=pl.BlockSpec((rows, dv), lambda b, h, t: (row_map(b, h, t), h)),
        out_shape=jax.ShapeDtypeStruct((bsz * seq, heads * dv), BF16),
        scratch_shapes=[pltpu.VMEM((dv, dk), F32)],
        compiler_params=_cparams("parallel", "parallel", "arbitrary"),
        name="gla",
    )(proj, proj, proj, proj, proj, wg, bg, ng)


def _xattn_body(q_ref, k_ref, v_ref, o_ref, *, heads, scale):
    dh = q_ref.shape[1] // heads
    for h in range(heads):
        sl = slice(dh * h, dh * (h + 1))
        s = _dot_nt(q_ref[:, sl], k_ref[:, sl]) * scale
        s = s - jnp.max(s, axis=-1, keepdims=True)
        p = jnp.exp(s)
        p = p / jnp.sum(p, axis=-1, keepdims=True)
        o_ref[:, sl] = _dot(p.astype(BF16), v_ref[:, sl]).astype(o_ref.dtype)


def _xattn(q, kv, *, seq, mem_len, tm):
    m, d = q.shape
    per_b = seq // tm
    return pl.pallas_call(
        functools.partial(_xattn_body, heads=XA_HEADS, scale=(d // XA_HEADS) ** -0.5),
        grid=(m // tm,),
        in_specs=[pl.BlockSpec((tm, d), lambda i: (i, 0)),
                  pl.BlockSpec((mem_len, d), lambda i: (i // per_b, 0)),
                  pl.BlockSpec((mem_len, d), lambda i: (i // per_b, 1))],
        out_specs=pl.BlockSpec((tm, d), lambda i: (i, 0)),
        out_shape=jax.ShapeDtypeStruct((m, d), BF16),
        compiler_params=_cparams("parallel"),
        name="xattn",
    )(q, kv, kv)


def _ffn_body(x_ref, g_ref, wv_ref, wg_ref, cwv_ref, cwg_ref, cbv_ref, cbg_ref, wo_ref, fg_ref,
              o_ref, xn_ref, act_prev, act_next, carry, *bufs, rows, n_i, n_j, tiles_per_seq, final_norm, sub):
    tm = x_ref.shape[0]
    tf = wv_ref.shape[1]
    n_sub = tf // sub
    k = cwv_ref.shape[0]
    s = pl.program_id(0)
    cur = jnp.minimum(s, n_i * n_j - 1)
    i = cur // n_j
    j = cur % n_j
    jp = jnp.maximum(s - 1, 0) % n_j

    def row_blocks(fn):
        def blk(r, c):
            fn(pl.ds(pl.multiple_of(r * rows, rows), rows))
            return c
        lax.fori_loop(0, tm // rows, blk, 0)

    @pl.when(s == 0)
    def _():
        act_next[...] = jnp.zeros(act_next.shape, BF16)
        o_ref[...] = jnp.zeros(o_ref.shape, F32)

    @pl.when(j == 0)
    def _():
        def norm(sl):
            xn_ref[sl, :] = _rms_rows(x_ref[sl, :], g_ref[...]).astype(BF16)
        row_blocks(norm)

    @pl.when((jp == 0) & (s > 0))
    def _():
        o_ref[...] = x_ref[...]

    @pl.when(i % tiles_per_seq == 0)
    def _():
        carry[j] = jnp.zeros(carry.shape[1:], F32)

    act_prev[...] = act_next[...]
    xn = xn_ref[...]
    for t in range(2 * n_sub):
        cs = slice(sub * (t % n_sub), sub * (t % n_sub + 1))
        bufs[t][pl.ds(0, SUBLANE), :] = carry[j, pl.ds(SUBLANE * (t // n_sub), SUBLANE), cs]

    def conv(w_ref, buf, cw_ref, cb_ref, cs):
        buf[pl.ds(SUBLANE, tm), :] = _dot(xn, w_ref[:, cs])
        y = cb_ref[:, cs] + cw_ref[k - 1:k, cs] * buf[pl.ds(SUBLANE, tm), :]
        for d in range(1, k):
            y = y + cw_ref[k - 1 - d:k - d, cs] * buf[pl.ds(SUBLANE - d, tm), :]
        return y

    for t in range(n_sub):
        cs = slice(sub * t, sub * (t + 1))
        val = conv(wv_ref, bufs[t], cwv_ref, cbv_ref, cs)
        gate = conv(wg_ref, bufs[n_sub + t], cwg_ref, cbg_ref, cs)
        act_next[:, cs] = (val * jax.nn.gelu(gate)).astype(BF16)
    o_ref[...] += _dot(act_prev[...], wo_ref[...])

    for t in range(2 * n_sub):
        cs = slice(sub * (t % n_sub), sub * (t % n_sub + 1))
        carry[j, pl.ds(SUBLANE * (t // n_sub), SUBLANE), cs] = bufs[t][pl.ds(tm, SUBLANE), :]

    if final_norm:
        @pl.when((jp == n_j - 1) & (s > 0))
        def _():
            def norm(sl):
                o_ref[sl, :] = _rms_rows(o_ref[sl, :], fg_ref[...])
            row_blocks(norm)


def _ffn(x, g, w_in, cw, cb, w_out, fg, *, seq, tm, tf, final_norm, sub=256):
    m, d = x.shape
    d_ff = w_out.shape[0]
    n_i = m // tm
    n_j = d_ff // tf
    assert n_j >= 2
    k = cw.shape[0]
    cb2 = cb.reshape(1, -1)
    last = n_i * n_j - 1
    ci = lambda s: jnp.minimum(s, last) // n_j
    cj = lambda s: jnp.minimum(s, last) % n_j
    pi = lambda s: jnp.maximum(s - 1, 0) // n_j
    pj = lambda s: jnp.maximum(s - 1, 0) % n_j
    return pl.pallas_call(
        functools.partial(_ffn_body, rows=min(tm, 256), n_i=n_i, n_j=n_j, tiles_per_seq=seq // tm,
                          final_norm=final_norm, sub=sub),
        grid=(n_i * n_j + 1,),
        in_specs=[pl.BlockSpec((tm, d), lambda s: (ci(s), 0)),
                  pl.BlockSpec((1, d), lambda s: (0, 0)),
                  pl.BlockSpec((d, tf), lambda s: (0, cj(s))),
                  pl.BlockSpec((d, tf), lambda s: (0, n_j + cj(s))),
                  pl.BlockSpec((k, tf), lambda s: (0, cj(s))),
                  pl.BlockSpec((k, tf), lambda s: (0, n_j + cj(s))),
                  pl.BlockSpec((1, tf), lambda s: (0, cj(s))),
                  pl.BlockSpec((1, tf), lambda s: (0, n_j + cj(s))),
                  pl.BlockSpec((tf, d), lambda s: (pj(s), 0)),
                  pl.BlockSpec((1, d), lambda s: (0, 0))],
        out_specs=pl.BlockSpec((tm, d), lambda s: (pi(s), 0)),
        out_shape=jax.ShapeDtypeStruct((m, d), F32),
        scratch_shapes=[pltpu.VMEM((tm, d), BF16),
                        pltpu.VMEM((tm, tf), BF16),
                        pltpu.VMEM((tm, tf), BF16),
                        pltpu.VMEM((n_j, 2 * SUBLANE, tf), F32)]
                       + [pltpu.VMEM((tm + SUBLANE, sub), F32) for _ in range(2 * tf // sub)],
        compiler_params=_cparams("arbitrary"),
        name="conv_ffn",
    )(x, g.reshape(1, d), w_in, w_in, cw, cw, cb2, cb2, w_out, fg.reshape(1, d))


def _pad_cols(w, n):
    return jnp.pad(w, ((0, 0), (0, n - w.shape[1])))


def _group_lanes(v, groups, heads):
    lead = v.shape[:-1]
    v = v.reshape(lead + (groups, heads))
    v = jnp.pad(v, [(0, 0)] * len(lead) + [(0, 0), (0, LANE - heads)])
    return v.reshape(lead + (groups * LANE,))


def kernel(x, mem, norm_mix_g, norm_cross_g, norm_ffn_g, ab_w_in, lru_conv_w, lru_conv_b, lru_w_a, lru_b_a, lru_w_x, lru_b_x, lru_lambda, ssd_conv_w, ssd_conv_b, ssd_dt_bias, ssd_a_log, ssd_d, ssd_norm_g, ab_w_out, gla_w_in, gla_w_gate_up, gla_b_gate, gla_norm_g, gla_w_out, mem_norm_g, xa_w_q, xa_w_kv, xa_w_o, ffn_w_in, ffn_conv_w, ffn_conv_b, ffn_w_out, final_norm_g):
    bsz, seq, d = x.shape
    mem_len = mem.shape[1]
    depth = norm_mix_g.shape[0]
    m = bsz * seq
    assert seq % 512 == 0 and d % 512 == 0

    w_a = lru_lambda.shape[1]
    h_b = ssd_dt_bias.shape[1]
    w_b = ssd_norm_g.shape[1]
    p_b = w_b // h_b
    conv_dim = ssd_conv_w.shape[2]
    g_b = 4
    n_b = (conv_dim - w_b) // (2 * g_b)
    r_b = h_b // g_b
    assert ab_w_in.shape[2] == 2 * w_a + w_b + conv_dim + h_b and r_b * p_b == 512 and n_b == LANE
    dk_c = gla_b_gate.shape[1]
    dv_c = gla_norm_g.shape[1]
    rank = gla_w_gate_up.shape[1]
    h_c = 4
    assert gla_w_in.shape[2] == 2 * dk_c + 2 * dv_c + rank and rank <= LANE

    x2 = x.reshape(m, d)
    kv_in = mem.reshape(bsz * mem_len, d)
    expand = (lax.broadcasted_iota(jnp.int32, (LANE, r_b * p_b), 1) // p_b
              == lax.broadcasted_iota(jnp.int32, (LANE, r_b * p_b), 0)).astype(BF16)

    for layer in range(depth):
        j = layer // 2
        if layer % 2 == 0:
            main = 2 * w_a + w_b + conv_dim
            w_in = jnp.concatenate([ab_w_in[j][:, :main], _group_lanes(ab_w_in[j][:, main:], g_b, r_b)], axis=1).astype(BF16)
            proj = _norm_matmul(x2, norm_mix_g[layer], w_in, tm=1024, tn=512, out_dtype=F32)
            y_a = _rglru(proj, lru_conv_w[j], lru_conv_b[j], lru_w_a[j].astype(BF16), lru_b_a[j],
                         lru_w_x[j].astype(BF16), lru_b_x[j], lru_lambda[j],
                         bsz=bsz, seq=seq, width=w_a, gate_col=0, x_col=w_a, rows=256, tc=512)
            y_b = _ssd(proj, ssd_conv_w[j], ssd_conv_b[j],
                       _group_lanes(ssd_dt_bias[j], g_b, r_b).reshape(1, -1),
                       _group_lanes(ssd_a_log[j], g_b, r_b).reshape(1, -1),
                       jnp.repeat(ssd_d[j], p_b).reshape(1, -1), ssd_norm_g[j].reshape(1, -1), expand,
                       bsz=bsz, seq=seq, groups=g_b, heads=r_b, hdim=p_b, nst=n_b,
                       z_col=2 * w_a, xs_col=2 * w_a + w_b, bm_col=2 * w_a + 2 * w_b,
                       cm_col=2 * w_a + 2 * w_b + g_b * n_b, dt_col=main, rows=256)
            w_out = ab_w_out[j].astype(BF16)
            x2 = _matmul_res([y_a, y_b], [w_out[:w_a], w_out[w_a:]], x2, tm=512, tn=1024)
        else:
            main = 2 * dk_c + 2 * dv_c
            w_in = _pad_cols(gla_w_in[j], main + LANE).astype(BF16)
            proj = _norm_matmul(x2, norm_mix_g[layer], w_in, tm=1024, tn=896, out_dtype=F32)
            wg = jnp.pad(gla_w_gate_up[j], ((0, LANE - rank), (0, 0))).astype(BF16)
            o = _gla(proj, wg, gla_b_gate[j].reshape(1, -1), gla_norm_g[j].reshape(1, -1),
                     bsz=bsz, seq=seq, heads=h_c, dk=dk_c // h_c, dv=dv_c // h_c,
                     q_col=0, k_col=dk_c, v_col=2 * dk_c, r_col=2 * dk_c + dv_c, gl_col=main, rows=256)
            x2 = _matmul_res([o], [gla_w_out[j].astype(BF16)], x2, tm=512, tn=d)

        q = _norm_matmul(x2, norm_cross_g[layer], xa_w_q[layer].astype(BF16), tm=1024, tn=1024, out_dtype=BF16)
        kv = _norm_matmul(kv_in, mem_norm_g, xa_w_kv[layer].astype(BF16), tm=bsz * mem_len, tn=1024, out_dtype=BF16)
        att = _xattn(q, kv, seq=seq, mem_len=mem_len, tm=512)
        x2 = _matmul_res([att], [xa_w_o[layer].astype(BF16)], x2, tm=512, tn=d)

        x2 = _ffn(x2, norm_ffn_g[layer], ffn_w_in[layer].astype(BF16), ffn_conv_w[layer], ffn_conv_b[layer],
                  ffn_w_out[layer].astype(BF16), final_norm_g, seq=seq, tm=512, tf=1024,
                  final_norm=(layer == depth - 1))
    return x2.reshape(bsz, seq, d)
```

```python
import functools
import math

import jax
import jax.numpy as jnp
from jax import lax
from jax.experimental import pallas as pl
from jax.experimental.pallas import tpu as pltpu

F32 = jnp.float32
BF16 = jnp.bfloat16

EPS = 1e-6
CHUNK = 64
SUB = 16
SSD_CHUNK = 128
HALF = 8
LRU_C = 8.0
GATE_TAU = 16.0
XA_HEADS = 4
LANE = 128
SUBLANE = 8
VMEM_LIMIT = 56 * 1024 * 1024


def _cparams(*sem):
    return pltpu.CompilerParams(dimension_semantics=sem, vmem_limit_bytes=VMEM_LIMIT)


def _dot(a, b):
    return jnp.dot(a, b, preferred_element_type=F32)


def _dot_nt(a, b):
    return lax.dot_general(a, b, (((1,), (1,)), ((), ())), preferred_element_type=F32)


def _dot_tn(a, b):
    return lax.dot_general(a, b, (((0,), (0,)), ((), ())), preferred_element_type=F32)


def _split3(x):
    hi = x.astype(BF16)
    r1 = x - hi.astype(F32)
    mid = r1.astype(BF16)
    lo = (r1 - mid.astype(F32)).astype(BF16)
    return hi, mid, lo


def _sel_dot_left(sel, x):
    hi, mid, lo = _split3(x)
    return _dot(sel, hi) + _dot(sel, mid) + _dot(sel, lo)


def _sel_dot_right(x, sel):
    hi, mid, lo = _split3(x)
    return _dot(hi, sel) + _dot(mid, sel) + _dot(lo, sel)


def _tri(n):
    i = lax.broadcasted_iota(jnp.int32, (n, n), 0)
    j = lax.broadcasted_iota(jnp.int32, (n, n), 1)
    return jnp.where(j <= i, 1.0, 0.0).astype(BF16)


def _sigmoid(x):
    return 0.5 * jnp.tanh(0.5 * x) + 0.5


def _silu(x):
    return x * _sigmoid(x)


def _rms_rows(x, g):
    ms = jnp.mean(x * x, axis=-1, keepdims=True)
    return x * lax.rsqrt(ms + EPS) * g


def _conv_tile(in_ref, buf, cw, cb, rows):
    k = cw.shape[0]
    x = in_ref[...]
    buf[pl.ds(SUBLANE, rows), :] = x
    y = cb + cw[k - 1:k, :] * x
    for s in range(1, k):
        y = y + cw[k - 1 - s:k - s, :] * buf[pl.ds(SUBLANE - s, rows), :]
    buf[pl.ds(0, SUBLANE), :] = buf[pl.ds(rows, SUBLANE), :]
    return y


def _norm_matmul_body(x_ref, g_ref, w_ref, o_ref, xn_ref, *, rows):
    tm = x_ref.shape[0]

    @pl.when(pl.program_id(1) == 0)
    def _():
        def blk(r, c):
            sl = pl.ds(pl.multiple_of(r * rows, rows), rows)
            xn_ref[sl, :] = _rms_rows(x_ref[sl, :], g_ref[...]).astype(BF16)
            return c
        lax.fori_loop(0, tm // rows, blk, 0)

    o_ref[...] = _dot(xn_ref[...], w_ref[...]).astype(o_ref.dtype)


def _norm_matmul(x, g, w, *, tm, tn, out_dtype):
    m, d = x.shape
    n = w.shape[1]
    return pl.pallas_call(
        functools.partial(_norm_matmul_body, rows=min(tm, 256)),
        grid=(m // tm, n // tn),
        in_specs=[pl.BlockSpec((tm, d), lambda i, j: (i, 0)),
                  pl.BlockSpec((1, d), lambda i, j: (0, 0)),
                  pl.BlockSpec((d, tn), lambda i, j: (0, j))],
        out_specs=pl.BlockSpec((tm, tn), lambda i, j: (i, j)),
        out_shape=jax.ShapeDtypeStruct((m, n), out_dtype),
        scratch_shapes=[pltpu.VMEM((tm, d), BF16)],
        compiler_params=_cparams("parallel", "arbitrary"),
        name="norm_matmul",
    )(x, g.reshape(1, d), w)


def _matmul_res_body(*refs, n_lhs):
    a_refs = refs[:n_lhs]
    w_refs = refs[n_lhs:2 * n_lhs]
    res_ref = refs[2 * n_lhs]
    o_ref = refs[2 * n_lhs + 1]
    acc = res_ref[...]
    for a_ref, w_ref in zip(a_refs, w_refs):
        acc = acc + _dot(a_ref[...], w_ref[...])
    o_ref[...] = acc


def _matmul_res(lhs, ws, res, *, tm, tn):
    m, n = res.shape
    n_lhs = len(lhs)
    in_specs = [pl.BlockSpec((tm, a.shape[1]), lambda j, i: (i, 0)) for a in lhs]
    in_specs += [pl.BlockSpec((w.shape[0], tn), lambda j, i: (0, j)) for w in ws]
    in_specs += [pl.BlockSpec((tm, tn), lambda j, i: (i, j))]
    return pl.pallas_call(
        functools.partial(_matmul_res_body, n_lhs=n_lhs),
        grid=(n // tn, m // tm),
        in_specs=in_specs,
        out_specs=pl.BlockSpec((tm, tn), lambda j, i: (i, j)),
        out_shape=jax.ShapeDtypeStruct((m, n), F32),
        compiler_params=_cparams("parallel", "arbitrary"),
        name="matmul_res",
    )(*lhs, *ws, res)


def _rglru_body(gate_ref, xa_ref, cw_ref, cb_ref, wa_ref, ba_ref, wx_ref, bx_ref, lam_ref,
                o_ref, xbuf, a_s, u_s, hc):
    rows, tc = xa_ref.shape

    @pl.when(pl.program_id(2) == 0)
    def _():
        xbuf[pl.ds(0, SUBLANE), :] = jnp.zeros((SUBLANE, tc), F32)
        hc[...] = jnp.zeros((SUBLANE, tc), F32)

    xc = _conv_tile(xa_ref, xbuf, cw_ref[...], cb_ref[...], rows)
    sp = jax.nn.softplus(-lam_ref[...])
    for h in range(tc // LANE):
        sl = slice(LANE * h, LANE * (h + 1))
        xh = xc[:, sl]
        xb = xh.astype(BF16)
        r = _sigmoid(_dot(xb, wa_ref[h]) + ba_ref[:, sl])
        i = _sigmoid(_dot(xb, wx_ref[h]) + bx_ref[:, sl])
        log_a = -LRU_C * r * sp[:, sl]
        th = jnp.tanh(log_a)
        a_s[:, sl] = jnp.exp(log_a)
        u_s[:, sl] = jnp.sqrt(-2.0 * th / (1.0 - th)) * (i * xh)

    row = lax.broadcasted_iota(jnp.int32, (SUBLANE, tc), 0)

    def step(g, h_prev):
        sl = pl.ds(pl.multiple_of(g * SUBLANE, SUBLANE), SUBLANE)
        a = a_s[sl, :]
        u = u_s[sl, :]
        for s in (1, 2, 4):
            keep = row >= s
            a_sh = jnp.where(keep, pltpu.roll(a, s, 0), 1.0)
            u_sh = jnp.where(keep, pltpu.roll(u, s, 0), 0.0)
            u = a * u_sh + u
            a = a * a_sh
        h = a * h_prev + u
        u_s[sl, :] = h
        return jnp.broadcast_to(h[SUBLANE - 1:SUBLANE, :], (SUBLANE, tc))

    hc[...] = lax.fori_loop(0, rows // SUBLANE, step, hc[...])
    o_ref[...] = (jax.nn.gelu(gate_ref[...]) * u_s[...]).astype(o_ref.dtype)


def _rglru(proj, cw, cb, wa, ba, wx, bx, lam, *, bsz, seq, width, gate_col, x_col, rows, tc):
    n_t = seq // rows
    n_c = width // tc
    hpt = tc // LANE
    row_map = lambda b, c, t: b * n_t + t
    vec = lambda: pl.BlockSpec((1, tc), lambda b, c, t: (0, c))
    return pl.pallas_call(
        _rglru_body,
        grid=(bsz, n_c, n_t),
        in_specs=[pl.BlockSpec((rows, tc), lambda b, c, t: (row_map(b, c, t), gate_col // tc + c)),
                  pl.BlockSpec((rows, tc), lambda b, c, t: (row_map(b, c, t), x_col // tc + c)),
                  pl.BlockSpec((cw.shape[0], tc), lambda b, c, t: (0, c)),
                  vec(),
                  pl.BlockSpec((hpt, LANE, LANE), lambda b, c, t: (c, 0, 0)),
                  vec(),
                  pl.BlockSpec((hpt, LANE, LANE), lambda b, c, t: (c, 0, 0)),
                  vec(), vec()],
        out_specs=pl.BlockSpec((rows, tc), lambda b, c, t: (row_map(b, c, t), c)),
        out_shape=jax.ShapeDtypeStruct((bsz * seq, width), BF16),
        scratch_shapes=[pltpu.VMEM((rows + SUBLANE, tc), F32),
                        pltpu.VMEM((rows, tc), F32),
                        pltpu.VMEM((rows, tc), F32),
                        pltpu.VMEM((SUBLANE, tc), F32)],
        compiler_params=_cparams("parallel", "parallel", "arbitrary"),
        name="rglru",
    )(proj, proj, cw, cb.reshape(1, -1), wa, ba.reshape(1, -1), wx, bx.reshape(1, -1), lam.reshape(1, -1))


def _ssd_body(xs_ref, bm_ref, cm_ref, dt_ref, z_ref, cwx_ref, cwb_ref, cwc_ref, cbx_ref, cbb_ref, cbc_ref,
              dtb_ref, alog_ref, d_ref, ng_ref, e_ref, o_ref, bufx, bufb, bufc, state, *, heads, hdim):
    rows, wg = xs_ref.shape
    nst = bm_ref.shape[1]

    @pl.when(pl.program_id(2) == 0)
    def _():
        bufx[pl.ds(0, SUBLANE), :] = jnp.zeros((SUBLANE, wg), F32)
        bufb[pl.ds(0, SUBLANE), :] = jnp.zeros((SUBLANE, nst), F32)
        bufc[pl.ds(0, SUBLANE), :] = jnp.zeros((SUBLANE, nst), F32)
        state[...] = jnp.zeros(state.shape, F32)

    xs_all = _silu(_conv_tile(xs_ref, bufx, cwx_ref[...], cbx_ref[...], rows))
    bm_all = _silu(_conv_tile(bm_ref, bufb, cwb_ref[...], cbb_ref[...], rows)).astype(BF16)
    cm_all = _silu(_conv_tile(cm_ref, bufc, cwc_ref[...], cbc_ref[...], rows)).astype(BF16)
    dt_all = jax.nn.softplus(dt_ref[...] + dtb_ref[...])
    dta_all = dt_all * (-jnp.exp(alog_ref[...]))
    expand = e_ref[...]
    dte_all = _sel_dot_right(dt_all, expand)
    ri = lax.broadcasted_iota(jnp.int32, (rows, rows), 0)
    rj = lax.broadcasted_iota(jnp.int32, (rows, rows), 1)
    tri = jnp.where((rj <= ri) & (rj // SSD_CHUNK == ri // SSD_CHUNK), 1.0, 0.0).astype(BF16)
    cs_all = _sel_dot_left(tri, dta_all)
    cse_all = _sel_dot_right(cs_all, expand)
    ii = lax.broadcasted_iota(jnp.int32, (SSD_CHUNK, SSD_CHUNK), 0)
    jj = lax.broadcasted_iota(jnp.int32, (SSD_CHUNK, SSD_CHUNK), 1)
    causal = jj <= ii

    for c in range(rows // SSD_CHUNK):
        sl = slice(SSD_CHUNK * c, SSD_CHUNK * (c + 1))
        xs = xs_all[sl]
        bm = bm_all[sl]
        cm = cm_all[sl]
        cs = cs_all[sl]
        cs_t = cs.T
        cse = cse_all[sl]
        cs_last = cse[SSD_CHUNK - 1:SSD_CHUNK, :]
        xdt = xs * dte_all[sl]
        cb = _dot_nt(cm, bm)
        parts = []
        for r in range(heads):
            seg = cs[:, r:r + 1] - cs_t[r:r + 1, :]
            lmat = jnp.exp(jnp.where(causal, seg, -jnp.inf))
            parts.append(_dot((cb * lmat).astype(BF16), xdt[:, r * hdim:(r + 1) * hdim].astype(BF16)))
        y = jnp.concatenate(parts, axis=1)
        st = state[...]
        y = y + _dot(cm, st.astype(BF16)) * jnp.exp(cse)
        state[...] = st * jnp.exp(cs_last) + _dot_tn(bm, (xdt * jnp.exp(cs_last - cse)).astype(BF16))
        y = y + d_ref[...] * xs
        y = y * _silu(z_ref[sl, :])
        y = y * lax.rsqrt(jnp.mean(y * y, axis=-1, keepdims=True) + EPS) * ng_ref[...]
        o_ref[sl, :] = y.astype(o_ref.dtype)


def _ssd(proj, cw, cb, dtb, alog, dexp, ng, expand, *, bsz, seq, groups, heads, hdim, nst,
         z_col, xs_col, bm_col, cm_col, dt_col, rows):
    n_t = seq // rows
    wg = heads * hdim
    width = groups * wg
    row_map = lambda b, g, t: b * n_t + t
    k = cw.shape[0]
    cb2 = cb.reshape(1, -1)

    def col(width_, base):
        return pl.BlockSpec((rows, width_), lambda b, g, t: (row_map(b, g, t), base // width_ + g))

    def par(nrow, width_, base):
        return pl.BlockSpec((nrow, width_), lambda b, g, t: (0, base // width_ + g))

    return pl.pallas_call(
        functools.partial(_ssd_body, heads=heads, hdim=hdim),
        grid=(bsz, groups, n_t),
        in_specs=[col(wg, xs_col), col(nst, bm_col), col(nst, cm_col), col(LANE, dt_col), col(wg, z_col),
                  par(k, wg, 0), par(k, nst, width), par(k, nst, width + groups * nst),
                  par(1, wg, 0), par(1, nst, width), par(1, nst, width + groups * nst),
                  par(1, LANE, 0), par(1, LANE, 0), par(1, wg, 0), par(1, wg, 0),
                  pl.BlockSpec((LANE, wg), lambda b, g, t: (0, 0))],
        out_specs=pl.BlockSpec((rows, wg), lambda b, g, t: (row_map(b, g, t), g)),
        out_shape=jax.ShapeDtypeStruct((bsz * seq, width), BF16),
        scratch_shapes=[pltpu.VMEM((rows + SUBLANE, wg), F32),
                        pltpu.VMEM((rows + SUBLANE, nst), F32),
                        pltpu.VMEM((rows + SUBLANE, nst), F32),
                        pltpu.VMEM((nst, wg), F32)],
        compiler_params=_cparams("parallel", "parallel", "arbitrary"),
        name="ssd",
    )(proj, proj, proj, proj, proj, cw, cw, cw, cb2, cb2, cb2, dtb, alog, dexp, ng, expand)


def _gla_body(q_ref, k_ref, v_ref, r_ref, gl_ref, wg_ref, bg_ref, ng_ref, o_ref, state, *, q_scale):
    rows, dk = q_ref.shape

    @pl.when(pl.program_id(2) == 0)
    def _():
        state[...] = jnp.zeros(state.shape, F32)

    la_all = jax.nn.log_sigmoid(_dot(gl_ref[...].astype(BF16), wg_ref[...]) + bg_ref[...]) / GATE_TAU
    tri = _tri(CHUNK)
    lane = lax.broadcasted_iota(jnp.int32, (HALF, CHUNK), 1)
    srow = lax.broadcasted_iota(jnp.int32, (HALF, CHUNK), 0)

    for c in range(rows // CHUNK):
        sl = slice(CHUNK * c, CHUNK * (c + 1))
        q = q_ref[sl, :] * q_scale
        k = k_ref[sl, :]
        v = v_ref[sl, :].astype(BF16)
        b = _sel_dot_left(tri, la_all[sl])
        b_last = b[CHUNK - 1:CHUNK, :]
        st = state[...]
        o = _dot_nt((q * jnp.exp(b)).astype(BF16), st.astype(BF16))
        state[...] = st * jnp.exp(b_last) + _dot_tn(v, (k * jnp.exp(b_last - b)).astype(BF16))

        att_rows = []
        for blk in range(CHUNK // SUB):
            r0 = SUB * blk
            qi, ki, bi = q[r0:r0 + SUB], k[r0:r0 + SUB], b[r0:r0 + SUB]
            halves = []
            for h0 in (0, HALF):
                qh, kh, bh = qi[h0:h0 + HALF], ki[h0:h0 + HALF], bi[h0:h0 + HALF]
                att = jnp.zeros((HALF, CHUNK), F32)
                for j in range(HALF):
                    e = jnp.exp(jnp.minimum(bh - bh[j:j + 1, :], 0.0))
                    s = jnp.sum(qh * kh[j:j + 1, :] * e, axis=-1, keepdims=True)
                    att = jnp.where(lane == r0 + h0 + j, s, att)
                halves.append(jnp.where(lane - (r0 + h0) <= srow, att, 0.0))
            bmid = bi[HALF - 1:HALF, :]
            q_lo = (qi[HALF:] * jnp.exp(bi[HALF:] - bmid)).astype(BF16)
            k_mid = jnp.concatenate([*([jnp.zeros((r0, dk), F32)] if r0 else []), ki[:HALF] * jnp.exp(bmid - bi[:HALF]),
                                     jnp.zeros((CHUNK - r0 - HALF, dk), F32)], axis=0).astype(BF16)
            blk_att = jnp.concatenate([halves[0], halves[1] + _dot_nt(q_lo, k_mid)], axis=0)
            if blk > 0:
                bref = b[r0 - 1:r0, :]
                qt = (qi * jnp.exp(bi - bref)).astype(BF16)
                kt = jnp.concatenate([k[:r0] * jnp.exp(bref - b[:r0]),
                                      jnp.zeros((CHUNK - r0, dk), F32)], axis=0).astype(BF16)
                blk_att = blk_att + _dot_nt(qt, kt)
            att_rows.append(blk_att)
        att = jnp.concatenate(att_rows, axis=0)
        o = o + _dot(att.astype(BF16), v)
        o = o * lax.rsqrt(jnp.mean(o * o, axis=-1, keepdims=True) + EPS)
        o = o * ng_ref[...] * _silu(r_ref[sl, :])
        o_ref[sl, :] = o.astype(o_ref.dtype)


def _gla(proj, wg, bg, ng, *, bsz, seq, heads, dk, dv, q_col, k_col, v_col, r_col, gl_col, rows):
    n_t = seq // rows
    row_map = lambda b, h, t: b * n_t + t

    def col(width_, base):
        return pl.BlockSpec((rows, width_), lambda b, h, t: (row_map(b, h, t), base // width_ + h))

    return pl.pallas_call(
        functools.partial(_gla_body, q_scale=dk ** -0.5),
        grid=(bsz, heads, n_t),
        in_specs=[col(dk, q_col), col(dk, k_col), col(dv, v_col), col(dv, r_col),
                  pl.BlockSpec((rows, LANE), lambda b, h, t: (row_map(b, h, t), gl_col // LANE)),
                  pl.BlockSpec((LANE, dk), lambda b, h, t: (0, h)),
                  pl.BlockSpec((1, dk), lambda b, h, t: (0, h)),
                  pl.BlockSpec((1, dv), lambda b, h, t: (0, h))],
        out_specs=pl.BlockSpec((rows, dv), lambda b, h, t: (row_map(b, h, t), h)),
        out_shape=jax.ShapeDtypeStruct((bsz * seq, heads * dv), BF16),
        scratch_shapes=[pltpu.VMEM((dv, dk), F32)],
        compiler_params=_cparams("parallel", "parallel", "arbitrary"),
        name="gla",
    )(proj, proj, proj, proj, proj, wg, bg, ng)


def _xattn_body(q_ref, k_ref, v_ref, o_ref, *, heads, scale):
    dh = q_ref.shape[1] // heads
    for h in range(heads):
        sl = slice(dh * h, dh * (h + 1))
        s = _dot_nt(q_ref[:, sl], k_ref[:, sl]) * scale
        s = s - jnp.max(s, axis=-1, keepdims=True)
        p = jnp.exp(s)
        p = p / jnp.sum(p, axis=-1, keepdims=True)
        o_ref[:, sl] = _dot(p.astype(BF16), v_ref[:, sl]).astype(o_ref.dtype)


def _xattn(q, kv, *, seq, mem_len, tm):
    m, d = q.shape
    per_b = seq // tm
    return pl.pallas_call(
        functools.partial(_xattn_body, heads=XA_HEADS, scale=(d // XA_HEADS) ** -0.5),
        grid=(m // tm,),
        in_specs=[pl.BlockSpec((tm, d), lambda i: (i, 0)),
                  pl.BlockSpec((mem_len, d), lambda i: (i // per_b, 0)),
                  pl.BlockSpec((mem_len, d), lambda i: (i // per_b, 1))],
        out_specs=pl.BlockSpec((tm, d), lambda i: (i, 0)),
        out_shape=jax.ShapeDtypeStruct((m, d), BF16),
        compiler_params=_cparams("parallel"),
        name="xattn",
    )(q, kv, kv)


def _ffn_body(x_ref, g_ref, wv_ref, wg_ref, cwv_ref, cwg_ref, cbv_ref, cbg_ref, wo_ref, fg_ref,
              o_ref, xn_ref, act_prev, act_next, carry, *bufs, rows, n_i, n_j, tiles_per_seq, final_norm, sub):
    tm = x_ref.shape[0]
    tf = wv_ref.shape[1]
    n_sub = tf // sub
    k = cwv_ref.shape[0]
    s = pl.program_id(0)
    cur = jnp.minimum(s, n_i * n_j - 1)
    i = cur // n_j
    j = cur % n_j
    jp = jnp.maximum(s - 1, 0) % n_j

    def row_blocks(fn):
        def blk(r, c):
            fn(pl.ds(pl.multiple_of(r * rows, rows), rows))
            return c
        lax.fori_loop(0, tm // rows, blk, 0)

    @pl.when(s == 0)
    def _():
        act_next[...] = jnp.zeros(act_next.shape, BF16)
        o_ref[...] = jnp.zeros(o_ref.shape, F32)

    @pl.when(j == 0)
    def _():
        def norm(sl):
            xn_ref[sl, :] = _rms_rows(x_ref[sl, :], g_ref[...]).astype(BF16)
        row_blocks(norm)

    @pl.when((jp == 0) & (s > 0))
    def _():
        o_ref[...] = x_ref[...]

    @pl.when(i % tiles_per_seq == 0)
    def _():
        carry[j] = jnp.zeros(carry.shape[1:], F32)

    act_prev[...] = act_next[...]
    xn = xn_ref[...]
    for t in range(2 * n_sub):
        cs = slice(sub * (t % n_sub), sub * (t % n_sub + 1))
        bufs[t][pl.ds(0, SUBLANE), :] = carry[j, pl.ds(SUBLANE * (t // n_sub), SUBLANE), cs]

    def conv(w_ref, buf, cw_ref, cb_ref, cs):
        buf[pl.ds(SUBLANE, tm), :] = _dot(xn, w_ref[:, cs])
        y = cb_ref[:, cs] + cw_ref[k - 1:k, cs] * buf[pl.ds(SUBLANE, tm), :]
        for d in range(1, k):
            y = y + cw_ref[k - 1 - d:k - d, cs] * buf[pl.ds(SUBLANE - d, tm), :]
        return y

    for t in range(n_sub):
        cs = slice(sub * t, sub * (t + 1))
        val = conv(wv_ref, bufs[t], cwv_ref, cbv_ref, cs)
        gate = conv(wg_ref, bufs[n_sub + t], cwg_ref, cbg_ref, cs)
        act_next[:, cs] = (val * jax.nn.gelu(gate)).astype(BF16)
    o_ref[...] += _dot(act_prev[...], wo_ref[...])

    for t in range(2 * n_sub):
        cs = slice(sub * (t % n_sub), sub * (t % n_sub + 1))
        carry[j, pl.ds(SUBLANE * (t // n_sub), SUBLANE), cs] = bufs[t][pl.ds(tm, SUBLANE), :]

    if final_norm:
        @pl.when((jp == n_j - 1) & (s > 0))
        def _():
            def norm(sl):
                o_ref[sl, :] = _rms_rows(o_ref[sl, :], fg_ref[...])
            row_blocks(norm)


def _ffn(x, g, w_in, cw, cb, w_out, fg, *, seq, tm, tf, final_norm, sub=256):
    m, d = x.shape
    d_ff = w_out.shape[0]
    n_i = m // tm
    n_j = d_ff // tf
    assert n_j >= 2
    k = cw.shape[0]
    cb2 = cb.reshape(1, -1)
    last = n_i * n_j - 1
    ci = lambda s: jnp.minimum(s, last) // n_j
    cj = lambda s: jnp.minimum(s, last) % n_j
    pi = lambda s: jnp.maximum(s - 1, 0) // n_j
    pj = lambda s: jnp.maximum(s - 1, 0) % n_j
    return pl.pallas_call(
        functools.partial(_ffn_body, rows=min(tm, 256), n_i=n_i, n_j=n_j, tiles_per_seq=seq // tm,
                          final_norm=final_norm, sub=sub),
        grid=(n_i * n_j + 1,),
        in_specs=[pl.BlockSpec((tm, d), lambda s: (ci(s), 0)),
                  pl.BlockSpec((1, d), lambda s: (0, 0)),
                  pl.BlockSpec((d, tf), lambda s: (0, cj(s))),
                  pl.BlockSpec((d, tf), lambda s: (0, n_j + cj(s))),
                  pl.BlockSpec((k, tf), lambda s: (0, cj(s))),
                  pl.BlockSpec((k, tf), lambda s: (0, n_j + cj(s))),
                  pl.BlockSpec((1, tf), lambda s: (0, cj(s))),
                  pl.BlockSpec((1, tf), lambda s: (0, n_j + cj(s))),
                  pl.BlockSpec((tf, d), lambda s: (pj(s), 0)),
                  pl.BlockSpec((1, d), lambda s: (0, 0))],
        out_specs=pl.BlockSpec((tm, d), lambda s: (pi(s), 0)),
        out_shape=jax.ShapeDtypeStruct((m, d), F32),
        scratch_shapes=[pltpu.VMEM((tm, d), BF16),
                        pltpu.VMEM((tm, tf), BF16),
                        pltpu.VMEM((tm, tf), BF16),
                        pltpu.VMEM((n_j, 2 * SUBLANE, tf), F32)]
                       + [pltpu.VMEM((tm + SUBLANE, sub), F32) for _ in range(2 * tf // sub)],
        compiler_params=_cparams("arbitrary"),
        name="conv_ffn",
    )(x, g.reshape(1, d), w_in, w_in, cw, cw, cb2, cb2, w_out, fg.reshape(1, d))


def _pad_cols(w, n):
    return jnp.pad(w, ((0, 0), (0, n - w.shape[1])))


def _group_lanes(v, groups, heads):
    lead = v.shape[:-1]
    v = v.reshape(lead + (groups, heads))
    v = jnp.pad(v, [(0, 0)] * len(lead) + [(0, 0), (0, LANE - heads)])
    return v.reshape(lead + (groups * LANE,))


def kernel(x, mem, norm_mix_g, norm_cross_g, norm_ffn_g, ab_w_in, lru_conv_w, lru_conv_b, lru_w_a, lru_b_a, lru_w_x, lru_b_x, lru_lambda, ssd_conv_w, ssd_conv_b, ssd_dt_bias, ssd_a_log, ssd_d, ssd_norm_g, ab_w_out, gla_w_in, gla_w_gate_up, gla_b_gate, gla_norm_g, gla_w_out, mem_norm_g, xa_w_q, xa_w_kv, xa_w_o, ffn_w_in, ffn_conv_w, ffn_conv_b, ffn_w_out, final_norm_g):
    bsz, seq, d = x.shape
    mem_len = mem.shape[1]
    depth = norm_mix_g.shape[0]
    m = bsz * seq
    assert seq % 512 == 0 and d % 512 == 0

    w_a = lru_lambda.shape[1]
    h_b = ssd_dt_bias.shape[1]
    w_b = ssd_norm_g.shape[1]
    p_b = w_b // h_b
    conv_dim = ssd_conv_w.shape[2]
    g_b = 4
    n_b = (conv_dim - w_b) // (2 * g_b)
    r_b = h_b // g_b
    assert ab_w_in.shape[2] == 2 * w_a + w_b + conv_dim + h_b and r_b * p_b == 512 and n_b == LANE
    dk_c = gla_b_gate.shape[1]
    dv_c = gla_norm_g.shape[1]
    rank = gla_w_gate_up.shape[1]
    h_c = 4
    assert gla_w_in.shape[2] == 2 * dk_c + 2 * dv_c + rank and rank <= LANE

    x2 = x.reshape(m, d)
    kv_in = mem.reshape(bsz * mem_len, d)
    expand = (lax.broadcasted_iota(jnp.int32, (LANE, r_b * p_b), 1) // p_b
              == lax.broadcasted_iota(jnp.int32, (LANE, r_b * p_b), 0)).astype(BF16)

    for layer in range(depth):
        j = layer // 2
        if layer % 2 == 0:
            main = 2 * w_a + w_b + conv_dim
            w_bf = ab_w_in[j].astype(BF16)
            n_proj = -(-(main + g_b * LANE) // 1024) * 1024
            w_in = _pad_cols(jnp.concatenate([w_bf[:, :main], _group_lanes(w_bf[:, main:], g_b, r_b)], axis=1), n_proj)
            proj = _norm_matmul(x2, norm_mix_g[layer], w_in, tm=1024, tn=1024, out_dtype=F32)
            y_a = _rglru(proj, lru_conv_w[j], lru_conv_b[j], lru_w_a[j].astype(BF16), lru_b_a[j],
                         lru_w_x[j].astype(BF16), lru_b_x[j], lru_lambda[j],
                         bsz=bsz, seq=seq, width=w_a, gate_col=0, x_col=w_a, rows=512, tc=512)
            y_b = _ssd(proj, ssd_conv_w[j], ssd_conv_b[j],
                       _group_lanes(ssd_dt_bias[j], g_b, r_b).reshape(1, -1),
                       _group_lanes(ssd_a_log[j], g_b, r_b).reshape(1, -1),
                       jnp.repeat(ssd_d[j], p_b).reshape(1, -1), ssd_norm_g[j].reshape(1, -1), expand,
                       bsz=bsz, seq=seq, groups=g_b, heads=r_b, hdim=p_b, nst=n_b,
                       z_col=2 * w_a, xs_col=2 * w_a + w_b, bm_col=2 * w_a + 2 * w_b,
                       cm_col=2 * w_a + 2 * w_b + g_b * n_b, dt_col=main, rows=256)
            w_out = ab_w_out[j].astype(BF16)
            x2 = _matmul_res([y_a, y_b], [w_out[:w_a], w_out[w_a:]], x2, tm=512, tn=1024)
        else:
            main = 2 * dk_c + 2 * dv_c
            w_in = _pad_cols(gla_w_in[j].astype(BF16), main + LANE)
            proj = _norm_matmul(x2, norm_mix_g[layer], w_in, tm=1024, tn=896, out_dtype=F32)
            wg = jnp.pad(gla_w_gate_up[j], ((0, LANE - rank), (0, 0))).astype(BF16)
            o = _gla(proj, wg, gla_b_gate[j].reshape(1, -1), gla_norm_g[j].reshape(1, -1),
                     bsz=bsz, seq=seq, heads=h_c, dk=dk_c // h_c, dv=dv_c // h_c,
                     q_col=0, k_col=dk_c, v_col=2 * dk_c, r_col=2 * dk_c + dv_c, gl_col=main, rows=512)
            x2 = _matmul_res([o], [gla_w_out[j].astype(BF16)], x2, tm=512, tn=d)

        q = _norm_matmul(x2, norm_cross_g[layer], xa_w_q[layer].astype(BF16), tm=1024, tn=1024, out_dtype=BF16)
        kv = _norm_matmul(kv_in, mem_norm_g, xa_w_kv[layer].astype(BF16), tm=bsz * mem_len, tn=1024, out_dtype=BF16)
        att = _xattn(q, kv, seq=seq, mem_len=mem_len, tm=512)
        x2 = _matmul_res([att], [xa_w_o[layer].astype(BF16)], x2, tm=512, tn=d)

        x2 = _ffn(x2, norm_ffn_g[layer], ffn_w_in[layer].astype(BF16), ffn_conv_w[layer], ffn_conv_b[layer],
                  ffn_w_out[layer].astype(BF16), final_norm_g, seq=seq, tm=512, tf=1024,
                  final_norm=(layer == depth - 1))
    return x2.reshape(bsz, seq, d)
```

```python
import functools
import math

import jax
import jax.numpy as jnp
from jax import lax
from jax.experimental import pallas as pl
from jax.experimental.pallas import tpu as pltpu

F32 = jnp.float32
BF16 = jnp.bfloat16

EPS = 1e-6
CHUNK = 64
SUB = 16
SSD_CHUNK = 128
HALF = 8
LRU_C = 8.0
GATE_TAU = 16.0
XA_HEADS = 4
LANE = 128
SUBLANE = 8
VMEM_LIMIT = 56 * 1024 * 1024


def _cparams(*sem):
    return pltpu.CompilerParams(dimension_semantics=sem, vmem_limit_bytes=VMEM_LIMIT)


def _dot(a, b):
    return jnp.dot(a, b, preferred_element_type=F32)


def _dot_nt(a, b):
    return lax.dot_general(a, b, (((1,), (1,)), ((), ())), preferred_element_type=F32)


def _dot_tn(a, b):
    return lax.dot_general(a, b, (((0,), (0,)), ((), ())), preferred_element_type=F32)


def _split3(x):
    hi = x.astype(BF16)
    r1 = x - hi.astype(F32)
    mid = r1.astype(BF16)
    lo = (r1 - mid.astype(F32)).astype(BF16)
    return hi, mid, lo


def _sel_dot_left(sel, x):
    hi, mid, lo = _split3(x)
    return _dot(sel, hi) + _dot(sel, mid) + _dot(sel, lo)


def _sel_dot_right(x, sel):
    hi, mid, lo = _split3(x)
    return _dot(hi, sel) + _dot(mid, sel) + _dot(lo, sel)


def _tri(n):
    i = lax.broadcasted_iota(jnp.int32, (n, n), 0)
    j = lax.broadcasted_iota(jnp.int32, (n, n), 1)
    return jnp.where(j <= i, 1.0, 0.0).astype(BF16)


def _sigmoid(x):
    return 0.5 * jnp.tanh(0.5 * x) + 0.5


def _silu(x):
    return x * _sigmoid(x)


def _rms_rows(x, g):
    ms = jnp.mean(x * x, axis=-1, keepdims=True)
    return x * lax.rsqrt(ms + EPS) * g


def _conv_tile(in_ref, buf, cw, cb, rows):
    k = cw.shape[0]
    x = in_ref[...]
    buf[pl.ds(SUBLANE, rows), :] = x
    y = cb + cw[k - 1:k, :] * x
    for s in range(1, k):
        y = y + cw[k - 1 - s:k - s, :] * buf[pl.ds(SUBLANE - s, rows), :]
    buf[pl.ds(0, SUBLANE), :] = buf[pl.ds(rows, SUBLANE), :]
    return y


def _norm_matmul_body(x_ref, g_ref, w_ref, *rest, rows, n_main):
    if n_main is None:
        o_ref, xn_ref = rest
    else:
        wt_ref, o_ref, xn_ref = rest
    tm = x_ref.shape[0]
    j = pl.program_id(1)

    @pl.when(j == 0)
    def _():
        def blk(r, c):
            sl = pl.ds(pl.multiple_of(r * rows, rows), rows)
            xn_ref[sl, :] = _rms_rows(x_ref[sl, :], g_ref[...]).astype(BF16)
            return c
        lax.fori_loop(0, tm // rows, blk, 0)

    if n_main is None:
        o_ref[...] = _dot(xn_ref[...], w_ref[...]).astype(o_ref.dtype)
    else:
        @pl.when(j < n_main)
        def _():
            o_ref[...] = _dot(xn_ref[...], w_ref[...]).astype(o_ref.dtype)

        @pl.when(j >= n_main)
        def _():
            o_ref[...] = _dot(xn_ref[...], wt_ref[...]).astype(o_ref.dtype)


def _norm_matmul(x, g, w, *, layer, tm, tn, out_dtype, w_tail=None):
    m, d = x.shape
    n_main = w.shape[2] // tn
    n_tail = 0 if w_tail is None else w_tail.shape[1] // tn
    assert w_tail is not None or w.shape[2] == n_main * tn
    in_specs = [pl.BlockSpec((tm, d), lambda i, j: (i, 0)),
                pl.BlockSpec((1, d), lambda i, j: (0, 0)),
                pl.BlockSpec((None, d, tn), lambda i, j: (layer, 0, jnp.minimum(j, n_main - 1)))]
    args = [x, g.reshape(1, d), w]
    if w_tail is not None:
        in_specs.append(pl.BlockSpec((d, tn), lambda i, j: (0, jnp.maximum(j - n_main, 0))))
        args.append(w_tail)
    return pl.pallas_call(
        functools.partial(_norm_matmul_body, rows=min(tm, 256), n_main=None if w_tail is None else n_main),
        grid=(m // tm, n_main + n_tail),
        in_specs=in_specs,
        out_specs=pl.BlockSpec((tm, tn), lambda i, j: (i, j)),
        out_shape=jax.ShapeDtypeStruct((m, (n_main + n_tail) * tn), out_dtype),
        scratch_shapes=[pltpu.VMEM((tm, d), BF16)],
        compiler_params=_cparams("parallel", "arbitrary"),
        name="norm_matmul",
    )(*args)


def _matmul_res_body(*refs, n_lhs):
    a_refs = refs[:n_lhs]
    w_refs = refs[n_lhs:2 * n_lhs]
    res_ref = refs[2 * n_lhs]
    o_ref = refs[2 * n_lhs + 1]
    acc = res_ref[...]
    for a_ref, w_ref in zip(a_refs, w_refs):
        acc = acc + _dot(a_ref[...], w_ref[...])
    o_ref[...] = acc


def _matmul_res(lhs, ws, res, *, tm, tn):
    m, n = res.shape
    n_lhs = len(lhs)
    in_specs = [pl.BlockSpec((tm, a.shape[1]), lambda j, i: (i, 0)) for a in lhs]
    for a, (w, layer, rb) in zip(lhs, ws):
        in_specs.append(pl.BlockSpec((None, a.shape[1], tn), lambda j, i, layer=layer, rb=rb: (layer, rb, j)))
    in_specs += [pl.BlockSpec((tm, tn), lambda j, i: (i, j))]
    return pl.pallas_call(
        functools.partial(_matmul_res_body, n_lhs=n_lhs),
        grid=(n // tn, m // tm),
        in_specs=in_specs,
        out_specs=pl.BlockSpec((tm, tn), lambda j, i: (i, j)),
        out_shape=jax.ShapeDtypeStruct((m, n), F32),
        compiler_params=_cparams("parallel", "arbitrary"),
        name="matmul_res",
    )(*lhs, *[w for w, _, _ in ws], res)


def _rglru_body(gate_ref, xa_ref, cw_ref, cb_ref, wa_ref, ba_ref, wx_ref, bx_ref, lam_ref,
                o_ref, xbuf, a_s, u_s, hc):
    rows, tc = xa_ref.shape

    @pl.when(pl.program_id(2) == 0)
    def _():
        xbuf[pl.ds(0, SUBLANE), :] = jnp.zeros((SUBLANE, tc), F32)
        hc[...] = jnp.zeros((SUBLANE, tc), F32)

    xc = _conv_tile(xa_ref, xbuf, cw_ref[...], cb_ref[...], rows)
    sp = jax.nn.softplus(-lam_ref[...])
    for h in range(tc // LANE):
        sl = slice(LANE * h, LANE * (h + 1))
        xh = xc[:, sl]
        xb = xh.astype(BF16)
        r = _sigmoid(_dot(xb, wa_ref[h]) + ba_ref[:, sl])
        i = _sigmoid(_dot(xb, wx_ref[h]) + bx_ref[:, sl])
        log_a = -LRU_C * r * sp[:, sl]
        th = jnp.tanh(log_a)
        a_s[:, sl] = jnp.exp(log_a)
        u_s[:, sl] = jnp.sqrt(-2.0 * th / (1.0 - th)) * (i * xh)

    row = lax.broadcasted_iota(jnp.int32, (SUBLANE, tc), 0)

    def step(g, h_prev):
        sl = pl.ds(pl.multiple_of(g * SUBLANE, SUBLANE), SUBLANE)
        a = a_s[sl, :]
        u = u_s[sl, :]
        for s in (1, 2, 4):
            keep = row >= s
            a_sh = jnp.where(keep, pltpu.roll(a, s, 0), 1.0)
            u_sh = jnp.where(keep, pltpu.roll(u, s, 0), 0.0)
            u = a * u_sh + u
            a = a * a_sh
        h = a * h_prev + u
        u_s[sl, :] = h
        return jnp.broadcast_to(h[SUBLANE - 1:SUBLANE, :], (SUBLANE, tc))

    hc[...] = lax.fori_loop(0, rows // SUBLANE, step, hc[...])
    o_ref[...] = (jax.nn.gelu(gate_ref[...]) * u_s[...]).astype(o_ref.dtype)


def _rglru(proj, cw, cb, wa, ba, wx, bx, lam, *, bsz, seq, width, gate_col, x_col, rows, tc):
    n_t = seq // rows
    n_c = width // tc
    hpt = tc // LANE
    row_map = lambda b, c, t: b * n_t + t
    vec = lambda: pl.BlockSpec((1, tc), lambda b, c, t: (0, c))
    return pl.pallas_call(
        _rglru_body,
        grid=(bsz, n_c, n_t),
        in_specs=[pl.BlockSpec((rows, tc), lambda b, c, t: (row_map(b, c, t), gate_col // tc + c)),
                  pl.BlockSpec((rows, tc), lambda b, c, t: (row_map(b, c, t), x_col // tc + c)),
                  pl.BlockSpec((cw.shape[0], tc), lambda b, c, t: (0, c)),
                  vec(),
                  pl.BlockSpec((hpt, LANE, LANE), lambda b, c, t: (c, 0, 0)),
                  vec(),
                  pl.BlockSpec((hpt, LANE, LANE), lambda b, c, t: (c, 0, 0)),
                  vec(), vec()],
        out_specs=pl.BlockSpec((rows, tc), lambda b, c, t: (row_map(b, c, t), c)),
        out_shape=jax.ShapeDtypeStruct((bsz * seq, width), BF16),
        scratch_shapes=[pltpu.VMEM((rows + SUBLANE, tc), F32),
                        pltpu.VMEM((rows, tc), F32),
                        pltpu.VMEM((rows, tc), F32),
                        pltpu.VMEM((SUBLANE, tc), F32)],
        compiler_params=_cparams("parallel", "parallel", "arbitrary"),
        name="rglru",
    )(proj, proj, cw, cb.reshape(1, -1), wa, ba.reshape(1, -1), wx, bx.reshape(1, -1), lam.reshape(1, -1))


def _ssd_body(xs_ref, bm_ref, cm_ref, dt_ref, z_ref, cwx_ref, cwb_ref, cwc_ref, cbx_ref, cbb_ref, cbc_ref,
              dtb_ref, alog_ref, d_ref, ng_ref, e_ref, o_ref, bufx, bufb, bufc, state, *, heads, hdim):
    rows, wg = xs_ref.shape
    nst = bm_ref.shape[1]

    @pl.when(pl.program_id(2) == 0)
    def _():
        bufx[pl.ds(0, SUBLANE), :] = jnp.zeros((SUBLANE, wg), F32)
        bufb[pl.ds(0, SUBLANE), :] = jnp.zeros((SUBLANE, nst), F32)
        bufc[pl.ds(0, SUBLANE), :] = jnp.zeros((SUBLANE, nst), F32)
        state[...] = jnp.zeros(state.shape, F32)

    xs_all = _silu(_conv_tile(xs_ref, bufx, cwx_ref[...], cbx_ref[...], rows))
    bm_all = _silu(_conv_tile(bm_ref, bufb, cwb_ref[...], cbb_ref[...], rows)).astype(BF16)
    cm_all = _silu(_conv_tile(cm_ref, bufc, cwc_ref[...], cbc_ref[...], rows)).astype(BF16)
    dt_all = jax.nn.softplus(dt_ref[...] + dtb_ref[...])
    dta_all = dt_all * (-jnp.exp(alog_ref[...]))
    expand = e_ref[...]
    dte_all = _sel_dot_right(dt_all, expand)
    ri = lax.broadcasted_iota(jnp.int32, (rows, rows), 0)
    rj = lax.broadcasted_iota(jnp.int32, (rows, rows), 1)
    tri = jnp.where((rj <= ri) & (rj // SSD_CHUNK == ri // SSD_CHUNK), 1.0, 0.0).astype(BF16)
    cs_all = _sel_dot_left(tri, dta_all)
    cse_all = _sel_dot_right(cs_all, expand)
    ii = lax.broadcasted_iota(jnp.int32, (SSD_CHUNK, SSD_CHUNK), 0)
    jj = lax.broadcasted_iota(jnp.int32, (SSD_CHUNK, SSD_CHUNK), 1)
    causal = jj <= ii

    for c in range(rows // SSD_CHUNK):
        sl = slice(SSD_CHUNK * c, SSD_CHUNK * (c + 1))
        xs = xs_all[sl]
        bm = bm_all[sl]
        cm = cm_all[sl]
        cs = cs_all[sl]
        cs_t = cs.T
        cse = cse_all[sl]
        cs_last = cse[SSD_CHUNK - 1:SSD_CHUNK, :]
        xdt = xs * dte_all[sl]
        cb = _dot_nt(cm, bm)
        parts = []
        for r in range(heads):
            seg = cs[:, r:r + 1] - cs_t[r:r + 1, :]
            lmat = jnp.exp(jnp.where(causal, seg, -jnp.inf))
            parts.append(_dot((cb * lmat).astype(BF16), xdt[:, r * hdim:(r + 1) * hdim].astype(BF16)))
        y = jnp.concatenate(parts, axis=1)
        st = state[...]
        y = y + _dot(cm, st.astype(BF16)) * jnp.exp(cse)
        state[...] = st * jnp.exp(cs_last) + _dot_tn(bm, (xdt * jnp.exp(cs_last - cse)).astype(BF16))
        y = y + d_ref[...] * xs
        y = y * _silu(z_ref[sl, :])
        y = y * lax.rsqrt(jnp.mean(y * y, axis=-1, keepdims=True) + EPS) * ng_ref[...]
        o_ref[sl, :] = y.astype(o_ref.dtype)


def _ssd(proj, cw, cb, dtb, alog, dexp, ng, expand, *, bsz, seq, groups, heads, hdim, nst,
         z_col, xs_col, bm_col, cm_col, dt_col, rows):
    n_t = seq // rows
    wg = heads * hdim
    width = groups * wg
    row_map = lambda b, g, t: b * n_t + t
    k = cw.shape[0]
    cb2 = cb.reshape(1, -1)

    def col(width_, base):
        return pl.BlockSpec((rows, width_), lambda b, g, t: (row_map(b, g, t), base // width_ + g))

    def par(nrow, width_, base):
        return pl.BlockSpec((nrow, width_), lambda b, g, t: (0, base // width_ + g))

    return pl.pallas_call(
        functools.partial(_ssd_body, heads=heads, hdim=hdim),
        grid=(bsz, groups, n_t),
        in_specs=[col(wg, xs_col), col(nst, bm_col), col(nst, cm_col), col(LANE, dt_col), col(wg, z_col),
                  par(k, wg, 0), par(k, nst, width), par(k, nst, width + groups * nst),
                  par(1, wg, 0), par(1, nst, width), par(1, nst, width + groups * nst),
                  par(1, LANE, 0), par(1, LANE, 0), par(1, wg, 0), par(1, wg, 0),
                  pl.BlockSpec((LANE, wg), lambda b, g, t: (0, 0))],
        out_specs=pl.BlockSpec((rows, wg), lambda b, g, t: (row_map(b, g, t), g)),
        out_shape=jax.ShapeDtypeStruct((bsz * seq, width), BF16),
        scratch_shapes=[pltpu.VMEM((rows + SUBLANE, wg), F32),
                        pltpu.VMEM((rows + SUBLANE, nst), F32),
                        pltpu.VMEM((rows + SUBLANE, nst), F32),
                        pltpu.VMEM((nst, wg), F32)],
        compiler_params=_cparams("parallel", "parallel", "arbitrary"),
        name="ssd",
    )(proj, proj, proj, proj, proj, cw, cw, cw, cb2, cb2, cb2, dtb, alog, dexp, ng, expand)


def _gla_body(q_ref, k_ref, v_ref, r_ref, gl_ref, wg_ref, bg_ref, ng_ref, o_ref, state, *, q_scale):
    rows, dk = q_ref.shape

    @pl.when(pl.program_id(2) == 0)
    def _():
        state[...] = jnp.zeros(state.shape, F32)

    la_all = jax.nn.log_sigmoid(_dot(gl_ref[...].astype(BF16), wg_ref[...]) + bg_ref[...]) / GATE_TAU
    tri = _tri(CHUNK)
    lane = lax.broadcasted_iota(jnp.int32, (HALF, CHUNK), 1)
    srow = lax.broadcasted_iota(jnp.int32, (HALF, CHUNK), 0)

    for c in range(rows // CHUNK):
        sl = slice(CHUNK * c, CHUNK * (c + 1))
        q = q_ref[sl, :] * q_scale
        k = k_ref[sl, :]
        v = v_ref[sl, :].astype(BF16)
        b = _sel_dot_left(tri, la_all[sl])
        b_last = b[CHUNK - 1:CHUNK, :]
        st = state[...]
        o = _dot_nt((q * jnp.exp(b)).astype(BF16), st.astype(BF16))
        state[...] = st * jnp.exp(b_last) + _dot_tn(v, (k * jnp.exp(b_last - b)).astype(BF16))

        att_rows = []
        for blk in range(CHUNK // SUB):
            r0 = SUB * blk
            qi, ki, bi = q[r0:r0 + SUB], k[r0:r0 + SUB], b[r0:r0 + SUB]
            halves = []
            for h0 in (0, HALF):
                qh, kh, bh = qi[h0:h0 + HALF], ki[h0:h0 + HALF], bi[h0:h0 + HALF]
                att = jnp.zeros((HALF, CHUNK), F32)
                for j in range(HALF):
                    e = jnp.exp(jnp.minimum(bh - bh[j:j + 1, :], 0.0))
                    s = jnp.sum(qh * kh[j:j + 1, :] * e, axis=-1, keepdims=True)
                    att = jnp.where(lane == r0 + h0 + j, s, att)
                halves.append(jnp.where(lane - (r0 + h0) <= srow, att, 0.0))
            bmid = bi[HALF - 1:HALF, :]
            q_lo = (qi[HALF:] * jnp.exp(bi[HALF:] - bmid)).astype(BF16)
            k_mid = jnp.concatenate([*([jnp.zeros((r0, dk), F32)] if r0 else []), ki[:HALF] * jnp.exp(bmid - bi[:HALF]),
                                     jnp.zeros((CHUNK - r0 - HALF, dk), F32)], axis=0).astype(BF16)
            blk_att = jnp.concatenate([halves[0], halves[1] + _dot_nt(q_lo, k_mid)], axis=0)
            if blk > 0:
                bref = b[r0 - 1:r0, :]
                qt = (qi * jnp.exp(bi - bref)).astype(BF16)
                kt = jnp.concatenate([k[:r0] * jnp.exp(bref - b[:r0]),
                                      jnp.zeros((CHUNK - r0, dk), F32)], axis=0).astype(BF16)
                blk_att = blk_att + _dot_nt(qt, kt)
            att_rows.append(blk_att)
        att = jnp.concatenate(att_rows, axis=0)
        o = o + _dot(att.astype(BF16), v)
        o = o * lax.rsqrt(jnp.mean(o * o, axis=-1, keepdims=True) + EPS)
        o = o * ng_ref[...] * _silu(r_ref[sl, :])
        o_ref[sl, :] = o.astype(o_ref.dtype)


def _gla(proj, wg, bg, ng, *, bsz, seq, heads, dk, dv, q_col, k_col, v_col, r_col, gl_col, rows):
    n_t = seq // rows
    row_map = lambda b, h, t: b * n_t + t

    def col(width_, base):
        return pl.BlockSpec((rows, width_), lambda b, h, t: (row_map(b, h, t), base // width_ + h))

    return pl.pallas_call(
        functools.partial(_gla_body, q_scale=dk ** -0.5),
        grid=(bsz, heads, n_t),
        in_specs=[col(dk, q_col), col(dk, k_col), col(dv, v_col), col(dv, r_col),
                  pl.BlockSpec((rows, LANE), lambda b, h, t: (row_map(b, h, t), gl_col // LANE)),
                  pl.BlockSpec((LANE, dk), lambda b, h, t: (0, h)),
                  pl.BlockSpec((1, dk), lambda b, h, t: (0, h)),
                  pl.BlockSpec((1, dv), lambda b, h, t: (0, h))],
        out_specs=pl.BlockSpec((rows, dv), lambda b, h, t: (row_map(b, h, t), h)),
        out_shape=jax.ShapeDtypeStruct((bsz * seq, heads * dv), BF16),
        scratch_shapes=[pltpu.VMEM((dv, dk), F32)],
        compiler_params=_cparams("parallel", "parallel", "arbitrary"),
        name="gla",
    )(proj, proj, proj, proj, proj, wg, bg, ng)


def _xattn_body(q_ref, k_ref, v_ref, o_ref, *, heads, scale):
    dh = q_ref.shape[1] // heads
    for h in range(heads):
        sl = slice(dh * h, dh * (h + 1))
        s = _dot_nt(q_ref[:, sl], k_ref[:, sl]) * scale
        s = s - jnp.max(s, axis=-1, keepdims=True)
        p = jnp.exp(s)
        p = p / jnp.sum(p, axis=-1, keepdims=True)
        o_ref[:, sl] = _dot(p.astype(BF16), v_ref[:, sl]).astype(o_ref.dtype)


def _xattn(q, kv, *, seq, mem_len, tm):
    m, d = q.shape
    per_b = seq // tm
    return pl.pallas_call(
        functools.partial(_xattn_body, heads=XA_HEADS, scale=(d // XA_HEADS) ** -0.5),
        grid=(m // tm,),
        in_specs=[pl.BlockSpec((tm, d), lambda i: (i, 0)),
                  pl.BlockSpec((mem_len, d), lambda i: (i // per_b, 0)),
                  pl.BlockSpec((mem_len, d), lambda i: (i // per_b, 1))],
        out_specs=pl.BlockSpec((tm, d), lambda i: (i, 0)),
        out_shape=jax.ShapeDtypeStruct((m, d), BF16),
        compiler_params=_cparams("parallel"),
        name="xattn",
    )(q, kv, kv)


def _ffn_body(x_ref, g_ref, wv_ref, wg_ref, cwv_ref, cwg_ref, wo_ref, fg_ref,
              o_ref, xn_ref, act_prev, act_next, carry, *bufs, rows, n_i, n_j, tiles_per_seq, final_norm, sub):
    tm = x_ref.shape[0]
    tf = wv_ref.shape[1]
    n_sub = tf // sub
    k = cwv_ref.shape[0] - 1
    s = pl.program_id(0)
    cur = jnp.minimum(s, n_i * n_j - 1)
    i = cur // n_j
    j = cur % n_j
    jp = jnp.maximum(s - 1, 0) % n_j

    def row_blocks(fn):
        def blk(r, c):
            fn(pl.ds(pl.multiple_of(r * rows, rows), rows))
            return c
        lax.fori_loop(0, tm // rows, blk, 0)

    @pl.when(s == 0)
    def _():
        act_next[...] = jnp.zeros(act_next.shape, BF16)
        o_ref[...] = jnp.zeros(o_ref.shape, F32)

    @pl.when(j == 0)
    def _():
        def norm(sl):
            xn_ref[sl, :] = _rms_rows(x_ref[sl, :], g_ref[...]).astype(BF16)
        row_blocks(norm)

    @pl.when((jp == 0) & (s > 0))
    def _():
        o_ref[...] = x_ref[...]

    @pl.when(i % tiles_per_seq == 0)
    def _():
        carry[j] = jnp.zeros(carry.shape[1:], F32)

    act_prev[...] = act_next[...]
    xn = xn_ref[...]
    for t in range(2 * n_sub):
        cs = slice(sub * (t % n_sub), sub * (t % n_sub + 1))
        bufs[t][pl.ds(0, SUBLANE), :] = carry[j, pl.ds(SUBLANE * (t // n_sub), SUBLANE), cs]

    def conv(w_ref, buf, cw_ref, cs):
        buf[pl.ds(SUBLANE, tm), :] = _dot(xn, w_ref[:, cs])
        y = cw_ref[k:k + 1, cs] + cw_ref[k - 1:k, cs] * buf[pl.ds(SUBLANE, tm), :]
        for d in range(1, k):
            y = y + cw_ref[k - 1 - d:k - d, cs] * buf[pl.ds(SUBLANE - d, tm), :]
        return y

    for t in range(n_sub):
        cs = slice(sub * t, sub * (t + 1))
        val = conv(wv_ref, bufs[t], cwv_ref, cs)
        gate = conv(wg_ref, bufs[n_sub + t], cwg_ref, cs)
        act_next[:, cs] = (val * jax.nn.gelu(gate)).astype(BF16)
    o_ref[...] += _dot(act_prev[...], wo_ref[...])

    for t in range(2 * n_sub):
        cs = slice(sub * (t % n_sub), sub * (t % n_sub + 1))
        carry[j, pl.ds(SUBLANE * (t // n_sub), SUBLANE), cs] = bufs[t][pl.ds(tm, SUBLANE), :]

    if final_norm:
        @pl.when((jp == n_j - 1) & (s > 0))
        def _():
            def norm(sl):
                o_ref[sl, :] = _rms_rows(o_ref[sl, :], fg_ref[...])
            row_blocks(norm)


def _ffn(x, g, w_in, cwb, w_out, fg, *, layer, seq, tm, tf, final_norm, sub=256):
    m, d = x.shape
    d_ff = w_out.shape[1]
    n_i = m // tm
    n_j = d_ff // tf
    assert n_j >= 2
    kb = cwb.shape[1]
    last = n_i * n_j - 1
    ci = lambda s: jnp.minimum(s, last) // n_j
    cj = lambda s: jnp.minimum(s, last) % n_j
    pi = lambda s: jnp.maximum(s - 1, 0) // n_j
    pj = lambda s: jnp.maximum(s - 1, 0) % n_j
    return pl.pallas_call(
        functools.partial(_ffn_body, rows=min(tm, 256), n_i=n_i, n_j=n_j, tiles_per_seq=seq // tm,
                          final_norm=final_norm, sub=sub),
        grid=(n_i * n_j + 1,),
        in_specs=[pl.BlockSpec((tm, d), lambda s: (ci(s), 0)),
                  pl.BlockSpec((1, d), lambda s: (0, 0)),
                  pl.BlockSpec((None, d, tf), lambda s: (layer, 0, cj(s))),
                  pl.BlockSpec((None, d, tf), lambda s: (layer, 0, n_j + cj(s))),
                  pl.BlockSpec((None, kb, tf), lambda s: (layer, 0, cj(s))),
                  pl.BlockSpec((None, kb, tf), lambda s: (layer, 0, n_j + cj(s))),
                  pl.BlockSpec((None, tf, d), lambda s: (layer, pj(s), 0)),
                  pl.BlockSpec((1, d), lambda s: (0, 0))],
        out_specs=pl.BlockSpec((tm, d), lambda s: (pi(s), 0)),
        out_shape=jax.ShapeDtypeStruct((m, d), F32),
        scratch_shapes=[pltpu.VMEM((tm, d), BF16),
                        pltpu.VMEM((tm, tf), BF16),
                        pltpu.VMEM((tm, tf), BF16),
                        pltpu.VMEM((n_j, 2 * SUBLANE, tf), F32)]
                       + [pltpu.VMEM((tm + SUBLANE, sub), F32) for _ in range(2 * tf // sub)],
        compiler_params=_cparams("arbitrary"),
        name="conv_ffn",
    )(x, g.reshape(1, d), w_in, w_in, cwb, cwb, w_out, fg.reshape(1, d))


def _pad_cols(w, n):
    return jnp.pad(w, ((0, 0), (0, n - w.shape[1])))


def _group_lanes(v, groups, heads):
    lead = v.shape[:-1]
    v = v.reshape(lead + (groups, heads))
    v = jnp.pad(v, [(0, 0)] * len(lead) + [(0, 0), (0, LANE - heads)])
    return v.reshape(lead + (groups * LANE,))


def kernel(x, mem, norm_mix_g, norm_cross_g, norm_ffn_g, ab_w_in, lru_conv_w, lru_conv_b, lru_w_a, lru_b_a, lru_w_x, lru_b_x, lru_lambda, ssd_conv_w, ssd_conv_b, ssd_dt_bias, ssd_a_log, ssd_d, ssd_norm_g, ab_w_out, gla_w_in, gla_w_gate_up, gla_b_gate, gla_norm_g, gla_w_out, mem_norm_g, xa_w_q, xa_w_kv, xa_w_o, ffn_w_in, ffn_conv_w, ffn_conv_b, ffn_w_out, final_norm_g):
    bsz, seq, d = x.shape
    mem_len = mem.shape[1]
    depth = norm_mix_g.shape[0]
    m = bsz * seq
    assert seq % 512 == 0 and d % 512 == 0

    w_a = lru_lambda.shape[1]
    h_b = ssd_dt_bias.shape[1]
    w_b = ssd_norm_g.shape[1]
    p_b = w_b // h_b
    conv_dim = ssd_conv_w.shape[2]
    g_b = 4
    n_b = (conv_dim - w_b) // (2 * g_b)
    r_b = h_b // g_b
    assert ab_w_in.shape[2] == 2 * w_a + w_b + conv_dim + h_b and r_b * p_b == 512 and n_b == LANE
    dk_c = gla_b_gate.shape[1]
    dv_c = gla_norm_g.shape[1]
    rank = gla_w_gate_up.shape[1]
    h_c = 4
    assert gla_w_in.shape[2] == 2 * dk_c + 2 * dv_c + rank and rank <= LANE

    x2 = x.reshape(m, d)
    kv_in = mem.reshape(bsz * mem_len, d)
    expand = (lax.broadcasted_iota(jnp.int32, (LANE, r_b * p_b), 1) // p_b
              == lax.broadcasted_iota(jnp.int32, (LANE, r_b * p_b), 0)).astype(BF16)
    ab_w_in_bf, ab_w_out_bf = ab_w_in.astype(BF16), ab_w_out.astype(BF16)
    gla_w_out_bf = gla_w_out.astype(BF16)
    xa_w_q_bf, xa_w_kv_bf, xa_w_o_bf = xa_w_q.astype(BF16), xa_w_kv.astype(BF16), xa_w_o.astype(BF16)
    ffn_w_in_bf, ffn_w_out_bf = ffn_w_in.astype(BF16), ffn_w_out.astype(BF16)
    ffn_cwb = jnp.concatenate([ffn_conv_w, ffn_conv_b[:, None, :]], axis=1)

    for layer in range(depth):
        j = layer // 2
        if layer % 2 == 0:
            main = 2 * w_a + w_b + conv_dim
            assert main % 1024 == 0
            w_tail = _pad_cols(_group_lanes(ab_w_in_bf[j][:, main:], g_b, r_b), 1024)
            proj = _norm_matmul(x2, norm_mix_g[layer], ab_w_in_bf, layer=j, tm=1024, tn=1024, out_dtype=F32,
                                w_tail=w_tail)
            y_a = _rglru(proj, lru_conv_w[j], lru_conv_b[j], lru_w_a[j].astype(BF16), lru_b_a[j],
                         lru_w_x[j].astype(BF16), lru_b_x[j], lru_lambda[j],
                         bsz=bsz, seq=seq, width=w_a, gate_col=0, x_col=w_a, rows=512, tc=512)
            y_b = _ssd(proj, ssd_conv_w[j], ssd_conv_b[j],
                       _group_lanes(ssd_dt_bias[j], g_b, r_b).reshape(1, -1),
                       _group_lanes(ssd_a_log[j], g_b, r_b).reshape(1, -1),
                       jnp.repeat(ssd_d[j], p_b).reshape(1, -1), ssd_norm_g[j].reshape(1, -1), expand,
                       bsz=bsz, seq=seq, groups=g_b, heads=r_b, hdim=p_b, nst=n_b,
                       z_col=2 * w_a, xs_col=2 * w_a + w_b, bm_col=2 * w_a + 2 * w_b,
                       cm_col=2 * w_a + 2 * w_b + g_b * n_b, dt_col=main, rows=256)
            x2 = _matmul_res([y_a, y_b], [(ab_w_out_bf, j, 0), (ab_w_out_bf, j, 1)], x2, tm=512, tn=1024)
        else:
            main = 2 * dk_c + 2 * dv_c
            w_in = _pad_cols(gla_w_in[j].astype(BF16), main + LANE)[None]
            proj = _norm_matmul(x2, norm_mix_g[layer], w_in, layer=0, tm=1024, tn=896, out_dtype=F32)
            wg = jnp.pad(gla_w_gate_up[j], ((0, LANE - rank), (0, 0))).astype(BF16)
            o = _gla(proj, wg, gla_b_gate[j].reshape(1, -1), gla_norm_g[j].reshape(1, -1),
                     bsz=bsz, seq=seq, heads=h_c, dk=dk_c // h_c, dv=dv_c // h_c,
                     q_col=0, k_col=dk_c, v_col=2 * dk_c, r_col=2 * dk_c + dv_c, gl_col=main, rows=512)
            x2 = _matmul_res([o], [(gla_w_out_bf, j, 0)], x2, tm=512, tn=d)

        q = _norm_matmul(x2, norm_cross_g[layer], xa_w_q_bf, layer=layer, tm=1024, tn=1024, out_dtype=BF16)
        kv = _norm_matmul(kv_in, mem_norm_g, xa_w_kv_bf, layer=layer, tm=bsz * mem_len, tn=1024, out_dtype=BF16)
        att = _xattn(q, kv, seq=seq, mem_len=mem_len, tm=512)
        x2 = _matmul_res([att], [(xa_w_o_bf, layer, 0)], x2, tm=512, tn=d)

        x2 = _ffn(x2, norm_ffn_g[layer], ffn_w_in_bf, ffn_cwb, ffn_w_out_bf, final_norm_g, layer=layer,
                  seq=seq, tm=512, tf=1024, final_norm=(layer == depth - 1))
    return x2.reshape(bsz, seq, d)
```

```python
import functools
import math

import jax
import jax.numpy as jnp
from jax import lax
from jax.experimental import pallas as pl
from jax.experimental.pallas import tpu as pltpu

F32 = jnp.float32
BF16 = jnp.bfloat16

EPS = 1e-6
CHUNK = 64
SUB = 16
SSD_CHUNK = 128
HALF = 8
LRU_C = 8.0
GATE_TAU = 16.0
XA_HEADS = 4
LANE = 128
SUBLANE = 8
VMEM_LIMIT = 56 * 1024 * 1024


def _cparams(*sem):
    return pltpu.CompilerParams(dimension_semantics=sem, vmem_limit_bytes=VMEM_LIMIT)


def _dot(a, b):
    return jnp.dot(a, b, preferred_element_type=F32)


def _dot_nt(a, b):
    return lax.dot_general(a, b, (((1,), (1,)), ((), ())), preferred_element_type=F32)


def _dot_tn(a, b):
    return lax.dot_general(a, b, (((0,), (0,)), ((), ())), preferred_element_type=F32)


def _split3(x):
    hi = x.astype(BF16)
    r1 = x - hi.astype(F32)
    mid = r1.astype(BF16)
    lo = (r1 - mid.astype(F32)).astype(BF16)
    return hi, mid, lo


def _sel_dot_left(sel, x):
    hi, mid, lo = _split3(x)
    return _dot(sel, hi) + _dot(sel, mid) + _dot(sel, lo)


def _sel_dot_right(x, sel):
    hi, mid, lo = _split3(x)
    return _dot(hi, sel) + _dot(mid, sel) + _dot(lo, sel)


def _tri(n):
    i = lax.broadcasted_iota(jnp.int32, (n, n), 0)
    j = lax.broadcasted_iota(jnp.int32, (n, n), 1)
    return jnp.where(j <= i, 1.0, 0.0).astype(BF16)


def _sigmoid(x):
    return 0.5 * jnp.tanh(0.5 * x) + 0.5


def _silu(x):
    return x * _sigmoid(x)


def _rms_rows(x, g):
    ms = jnp.mean(x * x, axis=-1, keepdims=True)
    return x * lax.rsqrt(ms + EPS) * g


def _conv_tile(in_ref, buf, cw, cb, rows):
    k = cw.shape[0]
    x = in_ref[...]
    buf[pl.ds(SUBLANE, rows), :] = x
    y = cb + cw[k - 1:k, :] * x
    for s in range(1, k):
        y = y + cw[k - 1 - s:k - s, :] * buf[pl.ds(SUBLANE - s, rows), :]
    buf[pl.ds(0, SUBLANE), :] = buf[pl.ds(rows, SUBLANE), :]
    return y


def _norm_matmul_body(x_ref, g_ref, w_ref, *rest, rows, has_tail):
    if has_tail:
        wt_ref, o_ref, ot_ref, xn_ref = rest
    else:
        o_ref, xn_ref = rest
    tm = x_ref.shape[0]

    @pl.when(pl.program_id(1) == 0)
    def _():
        def blk(r, c):
            sl = pl.ds(pl.multiple_of(r * rows, rows), rows)
            xn_ref[sl, :] = _rms_rows(x_ref[sl, :], g_ref[...]).astype(BF16)
            return c
        lax.fori_loop(0, tm // rows, blk, 0)
        if has_tail:
            ot_ref[...] = _dot(xn_ref[...], wt_ref[...]).astype(ot_ref.dtype)

    o_ref[...] = _dot(xn_ref[...], w_ref[...]).astype(o_ref.dtype)


def _norm_matmul(x, g, w, *, layer, tm, tn, out_dtype, w_tail=None):
    m, d = x.shape
    n_main = w.shape[2] // tn
    assert w_tail is not None or w.shape[2] == n_main * tn
    in_specs = [pl.BlockSpec((tm, d), lambda i, j: (i, 0)),
                pl.BlockSpec((1, d), lambda i, j: (0, 0)),
                pl.BlockSpec((None, d, tn), lambda i, j: (layer, 0, j))]
    out_specs = pl.BlockSpec((tm, tn), lambda i, j: (i, j))
    out_shape = jax.ShapeDtypeStruct((m, n_main * tn), out_dtype)
    args = [x, g.reshape(1, d), w]
    if w_tail is not None:
        n_t = w_tail.shape[1]
        in_specs.append(pl.BlockSpec((d, n_t), lambda i, j: (0, 0)))
        out_specs = [out_specs, pl.BlockSpec((tm, n_t), lambda i, j: (i, 0))]
        out_shape = [out_shape, jax.ShapeDtypeStruct((m, n_t), out_dtype)]
        args.append(w_tail)
    return pl.pallas_call(
        functools.partial(_norm_matmul_body, rows=min(tm, 256), has_tail=w_tail is not None),
        grid=(m // tm, n_main),
        in_specs=in_specs,
        out_specs=out_specs,
        out_shape=out_shape,
        scratch_shapes=[pltpu.VMEM((tm, d), BF16)],
        compiler_params=_cparams("parallel", "arbitrary"),
        name="norm_matmul",
    )(*args)


def _matmul_res_body(*refs, n_lhs):
    a_refs = refs[:n_lhs]
    w_refs = refs[n_lhs:2 * n_lhs]
    res_ref = refs[2 * n_lhs]
    o_ref = refs[2 * n_lhs + 1]
    acc = res_ref[...]
    for a_ref, w_ref in zip(a_refs, w_refs):
        acc = acc + _dot(a_ref[...], w_ref[...])
    o_ref[...] = acc


def _matmul_res(lhs, ws, res, *, tm, tn):
    m, n = res.shape
    n_lhs = len(lhs)
    in_specs = [pl.BlockSpec((tm, a.shape[1]), lambda j, i: (i, 0)) for a in lhs]
    for a, (w, layer, rb) in zip(lhs, ws):
        in_specs.append(pl.BlockSpec((None, a.shape[1], tn), lambda j, i, layer=layer, rb=rb: (layer, rb, j)))
    in_specs += [pl.BlockSpec((tm, tn), lambda j, i: (i, j))]
    return pl.pallas_call(
        functools.partial(_matmul_res_body, n_lhs=n_lhs),
        grid=(n // tn, m // tm),
        in_specs=in_specs,
        out_specs=pl.BlockSpec((tm, tn), lambda j, i: (i, j)),
        out_shape=jax.ShapeDtypeStruct((m, n), F32),
        compiler_params=_cparams("parallel", "arbitrary"),
        name="matmul_res",
    )(*lhs, *[w for w, _, _ in ws], res)


def _rglru_body(gate_ref, xa_ref, cw_ref, cb_ref, wa_ref, ba_ref, wx_ref, bx_ref, lam_ref,
                o_ref, xbuf, a_s, u_s, hc):
    rows, tc = xa_ref.shape

    @pl.when(pl.program_id(2) == 0)
    def _():
        xbuf[pl.ds(0, SUBLANE), :] = jnp.zeros((SUBLANE, tc), F32)
        hc[...] = jnp.zeros((SUBLANE, tc), F32)

    xc = _conv_tile(xa_ref, xbuf, cw_ref[...], cb_ref[...], rows)
    sp = jax.nn.softplus(-lam_ref[...])
    for h in range(tc // LANE):
        sl = slice(LANE * h, LANE * (h + 1))
        xh = xc[:, sl]
        xb = xh.astype(BF16)
        r = _sigmoid(_dot(xb, wa_ref[h]) + ba_ref[:, sl])
        i = _sigmoid(_dot(xb, wx_ref[h]) + bx_ref[:, sl])
        log_a = -LRU_C * r * sp[:, sl]
        th = jnp.tanh(log_a)
        a_s[:, sl] = jnp.exp(log_a)
        u_s[:, sl] = jnp.sqrt(-2.0 * th / (1.0 - th)) * (i * xh)

    row = lax.broadcasted_iota(jnp.int32, (SUBLANE, tc), 0)

    def step(g, h_prev):
        sl = pl.ds(pl.multiple_of(g * SUBLANE, SUBLANE), SUBLANE)
        a = a_s[sl, :]
        u = u_s[sl, :]
        for s in (1, 2, 4):
            keep = row >= s
            a_sh = jnp.where(keep, pltpu.roll(a, s, 0), 1.0)
            u_sh = jnp.where(keep, pltpu.roll(u, s, 0), 0.0)
            u = a * u_sh + u
            a = a * a_sh
        h = a * h_prev + u
        u_s[sl, :] = h
        return jnp.broadcast_to(h[SUBLANE - 1:SUBLANE, :], (SUBLANE, tc))

    hc[...] = lax.fori_loop(0, rows // SUBLANE, step, hc[...])
    o_ref[...] = (jax.nn.gelu(gate_ref[...]) * u_s[...]).astype(o_ref.dtype)


def _rglru(proj, cw, cb, wa, ba, wx, bx, lam, *, bsz, seq, width, gate_col, x_col, rows, tc):
    n_t = seq // rows
    n_c = width // tc
    hpt = tc // LANE
    row_map = lambda b, c, t: b * n_t + t
    vec = lambda: pl.BlockSpec((1, tc), lambda b, c, t: (0, c))
    return pl.pallas_call(
        _rglru_body,
        grid=(bsz, n_c, n_t),
        in_specs=[pl.BlockSpec((rows, tc), lambda b, c, t: (row_map(b, c, t), gate_col // tc + c)),
                  pl.BlockSpec((rows, tc), lambda b, c, t: (row_map(b, c, t), x_col // tc + c)),
                  pl.BlockSpec((cw.shape[0], tc), lambda b, c, t: (0, c)),
                  vec(),
                  pl.BlockSpec((hpt, LANE, LANE), lambda b, c, t: (c, 0, 0)),
                  vec(),
                  pl.BlockSpec((hpt, LANE, LANE), lambda b, c, t: (c, 0, 0)),
                  vec(), vec()],
        out_specs=pl.BlockSpec((rows, tc), lambda b, c, t: (row_map(b, c, t), c)),
        out_shape=jax.ShapeDtypeStruct((bsz * seq, width), BF16),
        scratch_shapes=[pltpu.VMEM((rows + SUBLANE, tc), F32),
                        pltpu.VMEM((rows, tc), F32),
                        pltpu.VMEM((rows, tc), F32),
                        pltpu.VMEM((SUBLANE, tc), F32)],
        compiler_params=_cparams("parallel", "parallel", "arbitrary"),
        name="rglru",
    )(proj, proj, cw, cb.reshape(1, -1), wa, ba.reshape(1, -1), wx, bx.reshape(1, -1), lam.reshape(1, -1))


def _ssd_body(xs_ref, bm_ref, cm_ref, dt_ref, z_ref, cwx_ref, cwb_ref, cwc_ref, cbx_ref, cbb_ref, cbc_ref,
              dtb_ref, alog_ref, d_ref, ng_ref, e_ref, o_ref, bufx, bufb, bufc, state, *, heads, hdim):
    rows, wg = xs_ref.shape
    nst = bm_ref.shape[1]

    @pl.when(pl.program_id(2) == 0)
    def _():
        bufx[pl.ds(0, SUBLANE), :] = jnp.zeros((SUBLANE, wg), F32)
        bufb[pl.ds(0, SUBLANE), :] = jnp.zeros((SUBLANE, nst), F32)
        bufc[pl.ds(0, SUBLANE), :] = jnp.zeros((SUBLANE, nst), F32)
        state[...] = jnp.zeros(state.shape, F32)

    xs_all = _silu(_conv_tile(xs_ref, bufx, cwx_ref[...], cbx_ref[...], rows))
    bm_all = _silu(_conv_tile(bm_ref, bufb, cwb_ref[...], cbb_ref[...], rows)).astype(BF16)
    cm_all = _silu(_conv_tile(cm_ref, bufc, cwc_ref[...], cbc_ref[...], rows)).astype(BF16)
    dt_all = jax.nn.softplus(dt_ref[...] + dtb_ref[...])
    dta_all = dt_all * (-jnp.exp(alog_ref[...]))
    expand = e_ref[...]
    dte_all = _sel_dot_right(dt_all, expand)
    ri = lax.broadcasted_iota(jnp.int32, (rows, rows), 0)
    rj = lax.broadcasted_iota(jnp.int32, (rows, rows), 1)
    tri = jnp.where((rj <= ri) & (rj // SSD_CHUNK == ri // SSD_CHUNK), 1.0, 0.0).astype(BF16)
    cs_all = _sel_dot_left(tri, dta_all)
    cse_all = _sel_dot_right(cs_all, expand)
    ii = lax.broadcasted_iota(jnp.int32, (SSD_CHUNK, SSD_CHUNK), 0)
    jj = lax.broadcasted_iota(jnp.int32, (SSD_CHUNK, SSD_CHUNK), 1)
    causal = jj <= ii

    for c in range(rows // SSD_CHUNK):
        sl = slice(SSD_CHUNK * c, SSD_CHUNK * (c + 1))
        xs = xs_all[sl]
        bm = bm_all[sl]
        cm = cm_all[sl]
        cs = cs_all[sl]
        cs_t = cs.T
        cse = cse_all[sl]
        cs_last = cse[SSD_CHUNK - 1:SSD_CHUNK, :]
        xdt = xs * dte_all[sl]
        cb = _dot_nt(cm, bm)
        parts = []
        for r in range(heads):
            seg = cs[:, r:r + 1] - cs_t[r:r + 1, :]
            lmat = jnp.exp(jnp.where(causal, seg, -jnp.inf))
            parts.append(_dot((cb * lmat).astype(BF16), xdt[:, r * hdim:(r + 1) * hdim].astype(BF16)))
        y = jnp.concatenate(parts, axis=1)
        st = state[...]
        y = y + _dot(cm, st.astype(BF16)) * jnp.exp(cse)
        state[...] = st * jnp.exp(cs_last) + _dot_tn(bm, (xdt * jnp.exp(cs_last - cse)).astype(BF16))
        y = y + d_ref[...] * xs
        y = y * _silu(z_ref[sl, :])
        y = y * lax.rsqrt(jnp.mean(y * y, axis=-1, keepdims=True) + EPS) * ng_ref[...]
        o_ref[sl, :] = y.astype(o_ref.dtype)


def _ssd(proj, dt_proj, cw, cb, dtb, alog, dexp, ng, expand, *, bsz, seq, groups, heads, hdim, nst,
         z_col, xs_col, bm_col, cm_col, rows):
    n_t = seq // rows
    wg = heads * hdim
    width = groups * wg
    row_map = lambda b, g, t: b * n_t + t
    k = cw.shape[0]
    cb2 = cb.reshape(1, -1)

    def col(width_, base):
        return pl.BlockSpec((rows, width_), lambda b, g, t: (row_map(b, g, t), base // width_ + g))

    def par(nrow, width_, base):
        return pl.BlockSpec((nrow, width_), lambda b, g, t: (0, base // width_ + g))

    return pl.pallas_call(
        functools.partial(_ssd_body, heads=heads, hdim=hdim),
        grid=(bsz, groups, n_t),
        in_specs=[col(wg, xs_col), col(nst, bm_col), col(nst, cm_col), col(LANE, 0), col(wg, z_col),
                  par(k, wg, 0), par(k, nst, width), par(k, nst, width + groups * nst),
                  par(1, wg, 0), par(1, nst, width), par(1, nst, width + groups * nst),
                  par(1, LANE, 0), par(1, LANE, 0), par(1, wg, 0), par(1, wg, 0),
                  pl.BlockSpec((LANE, wg), lambda b, g, t: (0, 0))],
        out_specs=pl.BlockSpec((rows, wg), lambda b, g, t: (row_map(b, g, t), g)),
        out_shape=jax.ShapeDtypeStruct((bsz * seq, width), BF16),
        scratch_shapes=[pltpu.VMEM((rows + SUBLANE, wg), F32),
                        pltpu.VMEM((rows + SUBLANE, nst), F32),
                        pltpu.VMEM((rows + SUBLANE, nst), F32),
                        pltpu.VMEM((nst, wg), F32)],
        compiler_params=_cparams("parallel", "parallel", "arbitrary"),
        name="ssd",
    )(proj, proj, proj, dt_proj, proj, cw, cw, cw, cb2, cb2, cb2, dtb, alog, dexp, ng, expand)


def _gla_body(q_ref, k_ref, v_ref, r_ref, gl_ref, wg_ref, bg_ref, ng_ref, o_ref, state, *, q_scale):
    rows, dk = q_ref.shape

    @pl.when(pl.program_id(2) == 0)
    def _():
        state[...] = jnp.zeros(state.shape, F32)

    la_all = jax.nn.log_sigmoid(_dot(gl_ref[...].astype(BF16), wg_ref[...]) + bg_ref[...]) / GATE_TAU
    tri = _tri(CHUNK)
    lane = lax.broadcasted_iota(jnp.int32, (HALF, CHUNK), 1)
    srow = lax.broadcasted_iota(jnp.int32, (HALF, CHUNK), 0)

    for c in range(rows // CHUNK):
        sl = slice(CHUNK * c, CHUNK * (c + 1))
        q = q_ref[sl, :] * q_scale
        k = k_ref[sl, :]
        v = v_ref[sl, :].astype(BF16)
        b = _sel_dot_left(tri, la_all[sl])
        b_last = b[CHUNK - 1:CHUNK, :]
        st = state[...]
        o = _dot_nt((q * jnp.exp(b)).astype(BF16), st.astype(BF16))
        state[...] = st * jnp.exp(b_last) + _dot_tn(v, (k * jnp.exp(b_last - b)).astype(BF16))

        att_rows = []
        for blk in range(CHUNK // SUB):
            r0 = SUB * blk
            qi, ki, bi = q[r0:r0 + SUB], k[r0:r0 + SUB], b[r0:r0 + SUB]
            halves = []
            for h0 in (0, HALF):
                qh, kh, bh = qi[h0:h0 + HALF], ki[h0:h0 + HALF], bi[h0:h0 + HALF]
                att = jnp.zeros((HALF, CHUNK), F32)
                for j in range(HALF):
                    e = jnp.exp(jnp.minimum(bh - bh[j:j + 1, :], 0.0))
                    s = jnp.sum(qh * kh[j:j + 1, :] * e, axis=-1, keepdims=True)
                    att = jnp.where(lane == r0 + h0 + j, s, att)
                halves.append(jnp.where(lane - (r0 + h0) <= srow, att, 0.0))
            bmid = bi[HALF - 1:HALF, :]
            q_lo = (qi[HALF:] * jnp.exp(bi[HALF:] - bmid)).astype(BF16)
            k_mid = jnp.concatenate([*([jnp.zeros((r0, dk), F32)] if r0 else []), ki[:HALF] * jnp.exp(bmid - bi[:HALF]),
                                     jnp.zeros((CHUNK - r0 - HALF, dk), F32)], axis=0).astype(BF16)
            blk_att = jnp.concatenate([halves[0], halves[1] + _dot_nt(q_lo, k_mid)], axis=0)
            if blk > 0:
                bref = b[r0 - 1:r0, :]
                qt = (qi * jnp.exp(bi - bref)).astype(BF16)
                kt = jnp.concatenate([k[:r0] * jnp.exp(bref - b[:r0]),
                                      jnp.zeros((CHUNK - r0, dk), F32)], axis=0).astype(BF16)
                blk_att = blk_att + _dot_nt(qt, kt)
            att_rows.append(blk_att)
        att = jnp.concatenate(att_rows, axis=0)
        o = o + _dot(att.astype(BF16), v)
        o = o * lax.rsqrt(jnp.mean(o * o, axis=-1, keepdims=True) + EPS)
        o = o * ng_ref[...] * _silu(r_ref[sl, :])
        o_ref[sl, :] = o.astype(o_ref.dtype)


def _gla(proj, gl_proj, wg, bg, ng, *, bsz, seq, heads, dk, dv, q_col, k_col, v_col, r_col, rows):
    n_t = seq // rows
    row_map = lambda b, h, t: b * n_t + t

    def col(width_, base):
        return pl.BlockSpec((rows, width_), lambda b, h, t: (row_map(b, h, t), base // width_ + h))

    return pl.pallas_call(
        functools.partial(_gla_body, q_scale=dk ** -0.5),
        grid=(bsz, heads, n_t),
        in_specs=[col(dk, q_col), col(dk, k_col), col(dv, v_col), col(dv, r_col),
                  pl.BlockSpec((rows, LANE), lambda b, h, t: (row_map(b, h, t), 0)),
                  pl.BlockSpec((LANE, dk), lambda b, h, t: (0, h)),
                  pl.BlockSpec((1, dk), lambda b, h, t: (0, h)),
                  pl.BlockSpec((1, dv), lambda b, h, t: (0, h))],
        out_specs=pl.BlockSpec((rows, dv), lambda b, h, t: (row_map(b, h, t), h)),
        out_shape=jax.ShapeDtypeStruct((bsz * seq, heads * dv), BF16),
        scratch_shapes=[pltpu.VMEM((dv, dk), F32)],
        compiler_params=_cparams("parallel", "parallel", "arbitrary"),
        name="gla",
    )(proj, proj, proj, proj, gl_proj, wg, bg, ng)


def _xattn_body(q_ref, k_ref, v_ref, o_ref, *, heads, scale):
    dh = q_ref.shape[1] // heads
    for h in range(heads):
        sl = slice(dh * h, dh * (h + 1))
        s = _dot_nt(q_ref[:, sl], k_ref[:, sl]) * scale
        s = s - jnp.max(s, axis=-1, keepdims=True)
        p = jnp.exp(s)
        p = p / jnp.sum(p, axis=-1, keepdims=True)
        o_ref[:, sl] = _dot(p.astype(BF16), v_ref[:, sl]).astype(o_ref.dtype)


def _xattn(q, kv, *, seq, mem_len, tm):
    m, d = q.shape
    per_b = seq // tm
    return pl.pallas_call(
        functools.partial(_xattn_body, heads=XA_HEADS, scale=(d // XA_HEADS) ** -0.5),
        grid=(m // tm,),
        in_specs=[pl.BlockSpec((tm, d), lambda i: (i, 0)),
                  pl.BlockSpec((mem_len, d), lambda i: (i // per_b, 0)),
                  pl.BlockSpec((mem_len, d), lambda i: (i // per_b, 1))],
        out_specs=pl.BlockSpec((tm, d), lambda i: (i, 0)),
        out_shape=jax.ShapeDtypeStruct((m, d), BF16),
        compiler_params=_cparams("parallel"),
        name="xattn",
    )(q, kv, kv)


def _ffn_body(x_ref, g_ref, wv_ref, wg_ref, cwv_ref, cwg_ref, wo_ref, fg_ref,
              o_ref, xn_ref, act_prev, act_next, carry, *bufs, rows, n_i, n_j, tiles_per_seq, final_norm, sub):
    tm = x_ref.shape[0]
    tf = wv_ref.shape[1]
    n_sub = tf // sub
    k = cwv_ref.shape[0] - 1
    s = pl.program_id(0)
    cur = jnp.minimum(s, n_i * n_j - 1)
    i = cur // n_j
    j = cur % n_j
    jp = jnp.maximum(s - 1, 0) % n_j

    def row_blocks(fn):
        def blk(r, c):
            fn(pl.ds(pl.multiple_of(r * rows, rows), rows))
            return c
        lax.fori_loop(0, tm // rows, blk, 0)

    @pl.when(s == 0)
    def _():
        act_next[...] = jnp.zeros(act_next.shape, BF16)
        o_ref[...] = jnp.zeros(o_ref.shape, F32)

    @pl.when(j == 0)
    def _():
        def norm(sl):
            xn_ref[sl, :] = _rms_rows(x_ref[sl, :], g_ref[...]).astype(BF16)
        row_blocks(norm)

    @pl.when((jp == 0) & (s > 0))
    def _():
        o_ref[...] = x_ref[...]

    @pl.when(i % tiles_per_seq == 0)
    def _():
        carry[j] = jnp.zeros(carry.shape[1:], F32)

    act_prev[...] = act_next[...]
    xn = xn_ref[...]
    for t in range(2 * n_sub):
        cs = slice(sub * (t % n_sub), sub * (t % n_sub + 1))
        bufs[t][pl.ds(0, SUBLANE), :] = carry[j, pl.ds(SUBLANE * (t // n_sub), SUBLANE), cs]

    def conv(w_ref, buf, cw_ref, cs):
        buf[pl.ds(SUBLANE, tm), :] = _dot(xn, w_ref[:, cs])
        y = cw_ref[k:k + 1, cs] + cw_ref[k - 1:k, cs] * buf[pl.ds(SUBLANE, tm), :]
        for d in range(1, k):
            y = y + cw_ref[k - 1 - d:k - d, cs] * buf[pl.ds(SUBLANE - d, tm), :]
        return y

    for t in range(n_sub):
        cs = slice(sub * t, sub * (t + 1))
        val = conv(wv_ref, bufs[t], cwv_ref, cs)
        gate = conv(wg_ref, bufs[n_sub + t], cwg_ref, cs)
        act_next[:, cs] = (val * jax.nn.gelu(gate)).astype(BF16)
    o_ref[...] += _dot(act_prev[...], wo_ref[...])

    for t in range(2 * n_sub):
        cs = slice(sub * (t % n_sub), sub * (t % n_sub + 1))
        carry[j, pl.ds(SUBLANE * (t // n_sub), SUBLANE), cs] = bufs[t][pl.ds(tm, SUBLANE), :]

    if final_norm:
        @pl.when((jp == n_j - 1) & (s > 0))
        def _():
            def norm(sl):
                o_ref[sl, :] = _rms_rows(o_ref[sl, :], fg_ref[...])
            row_blocks(norm)


def _ffn(x, g, w_in, cwb, w_out, fg, *, layer, seq, tm, tf, final_norm, sub=256):
    m, d = x.shape
    d_ff = w_out.shape[1]
    n_i = m // tm
    n_j = d_ff // tf
    assert n_j >= 2
    kb = cwb.shape[1]
    last = n_i * n_j - 1
    ci = lambda s: jnp.minimum(s, last) // n_j
    cj = lambda s: jnp.minimum(s, last) % n_j
    pi = lambda s: jnp.maximum(s - 1, 0) // n_j
    pj = lambda s: jnp.maximum(s - 1, 0) % n_j
    return pl.pallas_call(
        functools.partial(_ffn_body, rows=min(tm, 256), n_i=n_i, n_j=n_j, tiles_per_seq=seq // tm,
                          final_norm=final_norm, sub=sub),
        grid=(n_i * n_j + 1,),
        in_specs=[pl.BlockSpec((tm, d), lambda s: (ci(s), 0)),
                  pl.BlockSpec((1, d), lambda s: (0, 0)),
                  pl.BlockSpec((None, d, tf), lambda s: (layer, 0, cj(s))),
                  pl.BlockSpec((None, d, tf), lambda s: (layer, 0, n_j + cj(s))),
                  pl.BlockSpec((None, kb, tf), lambda s: (layer, 0, cj(s))),
                  pl.BlockSpec((None, kb, tf), lambda s: (layer, 0, n_j + cj(s))),
                  pl.BlockSpec((None, tf, d), lambda s: (layer, pj(s), 0)),
                  pl.BlockSpec((1, d), lambda s: (0, 0))],
        out_specs=pl.BlockSpec((tm, d), lambda s: (pi(s), 0)),
        out_shape=jax.ShapeDtypeStruct((m, d), F32),
        scratch_shapes=[pltpu.VMEM((tm, d), BF16),
                        pltpu.VMEM((tm, tf), BF16),
                        pltpu.VMEM((tm, tf), BF16),
                        pltpu.VMEM((n_j, 2 * SUBLANE, tf), F32)]
                       + [pltpu.VMEM((tm + SUBLANE, sub), F32) for _ in range(2 * tf // sub)],
        compiler_params=_cparams("arbitrary"),
        name="conv_ffn",
    )(x, g.reshape(1, d), w_in, w_in, cwb, cwb, w_out, fg.reshape(1, d))


def _pad_cols(w, n):
    return jnp.pad(w, ((0, 0), (0, n - w.shape[1])))


def _group_lanes(v, groups, heads):
    lead = v.shape[:-1]
    v = v.reshape(lead + (groups, heads))
    v = jnp.pad(v, [(0, 0)] * len(lead) + [(0, 0), (0, LANE - heads)])
    return v.reshape(lead + (groups * LANE,))


def kernel(x, mem, norm_mix_g, norm_cross_g, norm_ffn_g, ab_w_in, lru_conv_w, lru_conv_b, lru_w_a, lru_b_a, lru_w_x, lru_b_x, lru_lambda, ssd_conv_w, ssd_conv_b, ssd_dt_bias, ssd_a_log, ssd_d, ssd_norm_g, ab_w_out, gla_w_in, gla_w_gate_up, gla_b_gate, gla_norm_g, gla_w_out, mem_norm_g, xa_w_q, xa_w_kv, xa_w_o, ffn_w_in, ffn_conv_w, ffn_conv_b, ffn_w_out, final_norm_g):
    bsz, seq, d = x.shape
    mem_len = mem.shape[1]
    depth = norm_mix_g.shape[0]
    m = bsz * seq
    assert seq % 512 == 0 and d % 512 == 0

    w_a = lru_lambda.shape[1]
    h_b = ssd_dt_bias.shape[1]
    w_b = ssd_norm_g.shape[1]
    p_b = w_b // h_b
    conv_dim = ssd_conv_w.shape[2]
    g_b = 4
    n_b = (conv_dim - w_b) // (2 * g_b)
    r_b = h_b // g_b
    assert ab_w_in.shape[2] == 2 * w_a + w_b + conv_dim + h_b and r_b * p_b == 512 and n_b == LANE
    dk_c = gla_b_gate.shape[1]
    dv_c = gla_norm_g.shape[1]
    rank = gla_w_gate_up.shape[1]
    h_c = 4
    assert gla_w_in.shape[2] == 2 * dk_c + 2 * dv_c + rank and rank <= LANE

    x2 = x.reshape(m, d)
    kv_in = mem.reshape(bsz * mem_len, d)
    expand = (lax.broadcasted_iota(jnp.int32, (LANE, r_b * p_b), 1) // p_b
              == lax.broadcasted_iota(jnp.int32, (LANE, r_b * p_b), 0)).astype(BF16)
    ab_w_in_bf, ab_w_out_bf = ab_w_in.astype(BF16), ab_w_out.astype(BF16)
    gla_w_in_bf, gla_w_out_bf = gla_w_in.astype(BF16), gla_w_out.astype(BF16)
    xa_w_q_bf, xa_w_kv_bf, xa_w_o_bf = xa_w_q.astype(BF16), xa_w_kv.astype(BF16), xa_w_o.astype(BF16)
    ffn_w_in_bf, ffn_w_out_bf = ffn_w_in.astype(BF16), ffn_w_out.astype(BF16)
    ffn_cwb = jnp.concatenate([ffn_conv_w, ffn_conv_b[:, None, :]], axis=1)

    for layer in range(depth):
        j = layer // 2
        if layer % 2 == 0:
            main = 2 * w_a + w_b + conv_dim
            assert main % 1024 == 0
            w_tail = _group_lanes(ab_w_in_bf[j][:, main:], g_b, r_b)
            proj, dt_proj = _norm_matmul(x2, norm_mix_g[layer], ab_w_in_bf, layer=j, tm=1024, tn=1024,
                                         out_dtype=F32, w_tail=w_tail)
            y_a = _rglru(proj, lru_conv_w[j], lru_conv_b[j], lru_w_a[j].astype(BF16), lru_b_a[j],
                         lru_w_x[j].astype(BF16), lru_b_x[j], lru_lambda[j],
                         bsz=bsz, seq=seq, width=w_a, gate_col=0, x_col=w_a, rows=512, tc=512)
            y_b = _ssd(proj, dt_proj, ssd_conv_w[j], ssd_conv_b[j],
                       _group_lanes(ssd_dt_bias[j], g_b, r_b).reshape(1, -1),
                       _group_lanes(ssd_a_log[j], g_b, r_b).reshape(1, -1),
                       jnp.repeat(ssd_d[j], p_b).reshape(1, -1), ssd_norm_g[j].reshape(1, -1), expand,
                       bsz=bsz, seq=seq, groups=g_b, heads=r_b, hdim=p_b, nst=n_b,
                       z_col=2 * w_a, xs_col=2 * w_a + w_b, bm_col=2 * w_a + 2 * w_b,
                       cm_col=2 * w_a + 2 * w_b + g_b * n_b, rows=256)
            x2 = _matmul_res([y_a, y_b], [(ab_w_out_bf, j, 0), (ab_w_out_bf, j, 1)], x2, tm=512, tn=1024)
        else:
            main = 2 * dk_c + 2 * dv_c
            assert main % 1024 == 0
            proj, gl_proj = _norm_matmul(x2, norm_mix_g[layer], gla_w_in_bf, layer=j, tm=1024, tn=1024, out_dtype=F32,
                                         w_tail=_pad_cols(gla_w_in_bf[j][:, main:], LANE))
            wg = jnp.pad(gla_w_gate_up[j], ((0, LANE - rank), (0, 0))).astype(BF16)
            o = _gla(proj, gl_proj, wg, gla_b_gate[j].reshape(1, -1), gla_norm_g[j].reshape(1, -1),
                     bsz=bsz, seq=seq, heads=h_c, dk=dk_c // h_c, dv=dv_c // h_c,
                     q_col=0, k_col=dk_c, v_col=2 * dk_c, r_col=2 * dk_c + dv_c, rows=512)
            x2 = _matmul_res([o], [(gla_w_out_bf, j, 0)], x2, tm=512, tn=d)

        q = _norm_matmul(x2, norm_cross_g[layer], xa_w_q_bf, layer=layer, tm=1024, tn=1024, out_dtype=BF16)
        kv = _norm_matmul(kv_in, mem_norm_g, xa_w_kv_bf, layer=layer, tm=bsz * mem_len, tn=1024, out_dtype=BF16)
        att = _xattn(q, kv, seq=seq, mem_len=mem_len, tm=512)
        x2 = _matmul_res([att], [(xa_w_o_bf, layer, 0)], x2, tm=512, tn=d)

        x2 = _ffn(x2, norm_ffn_g[layer], ffn_w_in_bf, ffn_cwb, ffn_w_out_bf, final_norm_g, layer=layer,
                  seq=seq, tm=512, tf=1024, final_norm=(layer == depth - 1))
    return x2.reshape(bsz, seq, d)
```

```python
import functools
import math

import jax
import jax.numpy as jnp
from jax import lax
from jax.experimental import pallas as pl
from jax.experimental.pallas import tpu as pltpu

F32 = jnp.float32
BF16 = jnp.bfloat16

EPS = 1e-6
CHUNK = 64
SUB = 16
SSD_CHUNK = 128
HALF = 8
LRU_C = 8.0
GATE_TAU = 16.0
XA_HEADS = 4
LANE = 128
SUBLANE = 8
VMEM_LIMIT = 56 * 1024 * 1024


def _cparams(*sem):
    return pltpu.CompilerParams(dimension_semantics=sem, vmem_limit_bytes=VMEM_LIMIT)


def _dot(a, b):
    return jnp.dot(a, b, preferred_element_type=F32)


def _dot_nt(a, b):
    return lax.dot_general(a, b, (((1,), (1,)), ((), ())), preferred_element_type=F32)


def _dot_tn(a, b):
    return lax.dot_general(a, b, (((0,), (0,)), ((), ())), preferred_element_type=F32)


def _split3(x):
    hi = x.astype(BF16)
    r1 = x - hi.astype(F32)
    mid = r1.astype(BF16)
    lo = (r1 - mid.astype(F32)).astype(BF16)
    return hi, mid, lo


def _sel_dot_left(sel, x):
    hi, mid, lo = _split3(x)
    return _dot(sel, hi) + _dot(sel, mid) + _dot(sel, lo)


def _sel_dot_right(x, sel):
    hi, mid, lo = _split3(x)
    return _dot(hi, sel) + _dot(mid, sel) + _dot(lo, sel)


def _tri(n):
    i = lax.broadcasted_iota(jnp.int32, (n, n), 0)
    j = lax.broadcasted_iota(jnp.int32, (n, n), 1)
    return jnp.where(j <= i, 1.0, 0.0).astype(BF16)


def _sigmoid(x):
    return 0.5 * jnp.tanh(0.5 * x) + 0.5


def _silu(x):
    return x * _sigmoid(x)


def _rms_rows(x, g):
    ms = jnp.mean(x * x, axis=-1, keepdims=True)
    return x * lax.rsqrt(ms + EPS) * g


def _conv_tile(in_ref, buf, cw, cb, rows):
    k = cw.shape[0]
    x = in_ref[...]
    buf[pl.ds(SUBLANE, rows), :] = x
    y = cb + cw[k - 1:k, :] * x
    for s in range(1, k):
        y = y + cw[k - 1 - s:k - s, :] * buf[pl.ds(SUBLANE - s, rows), :]
    buf[pl.ds(0, SUBLANE), :] = buf[pl.ds(rows, SUBLANE), :]
    return y


def _norm_matmul_body(x_ref, g_ref, w_ref, *rest, rows, has_tail):
    if has_tail:
        wt_ref, o_ref, ot_ref, xn_ref = rest
    else:
        o_ref, xn_ref = rest
    tm = x_ref.shape[0]

    @pl.when(pl.program_id(1) == 0)
    def _():
        def blk(r, c):
            sl = pl.ds(pl.multiple_of(r * rows, rows), rows)
            xn_ref[sl, :] = _rms_rows(x_ref[sl, :], g_ref[...]).astype(BF16)
            return c
        lax.fori_loop(0, tm // rows, blk, 0)
        if has_tail:
            ot_ref[...] = _dot(xn_ref[...], wt_ref[...]).astype(ot_ref.dtype)

    o_ref[...] = _dot(xn_ref[...], w_ref[...]).astype(o_ref.dtype)


def _norm_matmul(x, g, w, *, layer, tm, tn, out_dtype, w_tail=None):
    m, d = x.shape
    n_main = w.shape[2] // tn
    assert w_tail is not None or w.shape[2] == n_main * tn
    in_specs = [pl.BlockSpec((tm, d), lambda i, j: (i, 0)),
                pl.BlockSpec((1, d), lambda i, j: (0, 0)),
                pl.BlockSpec((None, d, tn), lambda i, j: (layer, 0, j))]
    out_specs = pl.BlockSpec((tm, tn), lambda i, j: (i, j))
    out_shape = jax.ShapeDtypeStruct((m, n_main * tn), out_dtype)
    args = [x, g.reshape(1, d), w]
    if w_tail is not None:
        n_t = w_tail.shape[1]
        in_specs.append(pl.BlockSpec((d, n_t), lambda i, j: (0, 0)))
        out_specs = [out_specs, pl.BlockSpec((tm, n_t), lambda i, j: (i, 0))]
        out_shape = [out_shape, jax.ShapeDtypeStruct((m, n_t), out_dtype)]
        args.append(w_tail)
    return pl.pallas_call(
        functools.partial(_norm_matmul_body, rows=min(tm, 256), has_tail=w_tail is not None),
        grid=(m // tm, n_main),
        in_specs=in_specs,
        out_specs=out_specs,
        out_shape=out_shape,
        scratch_shapes=[pltpu.VMEM((tm, d), BF16)],
        compiler_params=_cparams("parallel", "arbitrary"),
        name="norm_matmul",
    )(*args)


def _matmul_res_body(*refs, n_lhs):
    a_refs = refs[:n_lhs]
    w_refs = refs[n_lhs:2 * n_lhs]
    res_ref = refs[2 * n_lhs]
    o_ref = refs[2 * n_lhs + 1]
    acc = res_ref[...]
    for a_ref, w_ref in zip(a_refs, w_refs):
        acc = acc + _dot(a_ref[...], w_ref[...])
    o_ref[...] = acc


def _matmul_res(lhs, ws, res, *, tm, tn):
    m, n = res.shape
    n_lhs = len(lhs)
    in_specs = [pl.BlockSpec((tm, a.shape[1]), lambda j, i: (i, 0)) for a in lhs]
    for a, (w, layer, rb) in zip(lhs, ws):
        in_specs.append(pl.BlockSpec((None, a.shape[1], tn), lambda j, i, layer=layer, rb=rb: (layer, rb, j)))
    in_specs += [pl.BlockSpec((tm, tn), lambda j, i: (i, j))]
    return pl.pallas_call(
        functools.partial(_matmul_res_body, n_lhs=n_lhs),
        grid=(n // tn, m // tm),
        in_specs=in_specs,
        out_specs=pl.BlockSpec((tm, tn), lambda j, i: (i, j)),
        out_shape=jax.ShapeDtypeStruct((m, n), F32),
        compiler_params=_cparams("parallel", "arbitrary"),
        name="matmul_res",
    )(*lhs, *[w for w, _, _ in ws], res)


def _rglru_body(gate_ref, xa_ref, cw_ref, cb_ref, wa_ref, ba_ref, wx_ref, bx_ref, lam_ref,
                o_ref, xbuf, a_s, u_s, hc):
    rows, tc = xa_ref.shape

    @pl.when(pl.program_id(2) == 0)
    def _():
        xbuf[pl.ds(0, SUBLANE), :] = jnp.zeros((SUBLANE, tc), F32)
        hc[...] = jnp.zeros((SUBLANE, tc), F32)

    xc = _conv_tile(xa_ref, xbuf, cw_ref[...], cb_ref[...], rows)
    sp = jax.nn.softplus(-lam_ref[...])
    for h in range(tc // LANE):
        sl = slice(LANE * h, LANE * (h + 1))
        xh = xc[:, sl]
        xb = xh.astype(BF16)
        r = _sigmoid(_dot(xb, wa_ref[h]) + ba_ref[:, sl])
        i = _sigmoid(_dot(xb, wx_ref[h]) + bx_ref[:, sl])
        log_a = -LRU_C * r * sp[:, sl]
        th = jnp.tanh(log_a)
        a_s[:, sl] = jnp.exp(log_a)
        u_s[:, sl] = jnp.sqrt(-2.0 * th / (1.0 - th)) * (i * xh)

    row = lax.broadcasted_iota(jnp.int32, (SUBLANE, tc), 0)

    def step(g, h_prev):
        sl = pl.ds(pl.multiple_of(g * SUBLANE, SUBLANE), SUBLANE)
        a = a_s[sl, :]
        u = u_s[sl, :]
        for s in (1, 2, 4):
            keep = row >= s
            a_sh = jnp.where(keep, pltpu.roll(a, s, 0), 1.0)
            u_sh = jnp.where(keep, pltpu.roll(u, s, 0), 0.0)
            u = a * u_sh + u
            a = a * a_sh
        h = a * h_prev + u
        u_s[sl, :] = h
        return jnp.broadcast_to(h[SUBLANE - 1:SUBLANE, :], (SUBLANE, tc))

    hc[...] = lax.fori_loop(0, rows // SUBLANE, step, hc[...])
    o_ref[...] = (jax.nn.gelu(gate_ref[...]) * u_s[...]).astype(o_ref.dtype)


def _rglru(proj, cw, cb, wa, ba, wx, bx, lam, *, bsz, seq, width, gate_col, x_col, rows, tc):
    n_t = seq // rows
    n_c = width // tc
    hpt = tc // LANE
    row_map = lambda b, c, t: b * n_t + t
    vec = lambda: pl.BlockSpec((1, tc), lambda b, c, t: (0, c))
    return pl.pallas_call(
        _rglru_body,
        grid=(bsz, n_c, n_t),
        in_specs=[pl.BlockSpec((rows, tc), lambda b, c, t: (row_map(b, c, t), gate_col // tc + c)),
                  pl.BlockSpec((rows, tc), lambda b, c, t: (row_map(b, c, t), x_col // tc + c)),
                  pl.BlockSpec((cw.shape[0], tc), lambda b, c, t: (0, c)),
                  vec(),
                  pl.BlockSpec((hpt, LANE, LANE), lambda b, c, t: (c, 0, 0)),
                  vec(),
                  pl.BlockSpec((hpt, LANE, LANE), lambda b, c, t: (c, 0, 0)),
                  vec(), vec()],
        out_specs=pl.BlockSpec((rows, tc), lambda b, c, t: (row_map(b, c, t), c)),
        out_shape=jax.ShapeDtypeStruct((bsz * seq, width), BF16),
        scratch_shapes=[pltpu.VMEM((rows + SUBLANE, tc), F32),
                        pltpu.VMEM((rows, tc), F32),
                        pltpu.VMEM((rows, tc), F32),
                        pltpu.VMEM((SUBLANE, tc), F32)],
        compiler_params=_cparams("parallel", "parallel", "arbitrary"),
        name="rglru",
    )(proj, proj, cw, cb.reshape(1, -1), wa, ba.reshape(1, -1), wx, bx.reshape(1, -1), lam.reshape(1, -1))


def _ssd_body(xs_ref, bm_ref, cm_ref, dt_ref, z_ref, cwx_ref, cwb_ref, cwc_ref, cbx_ref, cbb_ref, cbc_ref,
              dtb_ref, alog_ref, d_ref, ng_ref, e_ref, o_ref, bufx, bufb, bufc, state, *, heads, hdim):
    rows, wg = xs_ref.shape
    nst = bm_ref.shape[1]

    @pl.when(pl.program_id(2) == 0)
    def _():
        bufx[pl.ds(0, SUBLANE), :] = jnp.zeros((SUBLANE, wg), F32)
        bufb[pl.ds(0, SUBLANE), :] = jnp.zeros((SUBLANE, nst), F32)
        bufc[pl.ds(0, SUBLANE), :] = jnp.zeros((SUBLANE, nst), F32)
        state[...] = jnp.zeros(state.shape, F32)

    xs_all = _silu(_conv_tile(xs_ref, bufx, cwx_ref[...], cbx_ref[...], rows))
    bm_all = _silu(_conv_tile(bm_ref, bufb, cwb_ref[...], cbb_ref[...], rows)).astype(BF16)
    cm_all = _silu(_conv_tile(cm_ref, bufc, cwc_ref[...], cbc_ref[...], rows)).astype(BF16)
    dt_all = jax.nn.softplus(dt_ref[...] + dtb_ref[...])
    dta_all = dt_all * (-jnp.exp(alog_ref[...]))
    expand = e_ref[...]
    dte_all = _sel_dot_right(dt_all, expand)
    ri = lax.broadcasted_iota(jnp.int32, (rows, rows), 0)
    rj = lax.broadcasted_iota(jnp.int32, (rows, rows), 1)
    tri = jnp.where((rj <= ri) & (rj // SSD_CHUNK == ri // SSD_CHUNK), 1.0, 0.0).astype(BF16)
    cs_all = _sel_dot_left(tri, dta_all)
    cse_all = _sel_dot_right(cs_all, expand)
    ii = lax.broadcasted_iota(jnp.int32, (SSD_CHUNK, SSD_CHUNK), 0)
    jj = lax.broadcasted_iota(jnp.int32, (SSD_CHUNK, SSD_CHUNK), 1)
    causal = jj <= ii

    for c in range(rows // SSD_CHUNK):
        sl = slice(SSD_CHUNK * c, SSD_CHUNK * (c + 1))
        xs = xs_all[sl]
        bm = bm_all[sl]
        cm = cm_all[sl]
        cs = cs_all[sl]
        cs_t = cs.T
        cse = cse_all[sl]
        cs_last = cse[SSD_CHUNK - 1:SSD_CHUNK, :]
        xdt = xs * dte_all[sl]
        cb = _dot_nt(cm, bm)
        parts = []
        for r in range(heads):
            seg = cs[:, r:r + 1] - cs_t[r:r + 1, :]
            lmat = jnp.exp(jnp.where(causal, seg, -jnp.inf))
            parts.append(_dot((cb * lmat).astype(BF16), xdt[:, r * hdim:(r + 1) * hdim].astype(BF16)))
        y = jnp.concatenate(parts, axis=1)
        st = state[...]
        y = y + _dot(cm, st.astype(BF16)) * jnp.exp(cse)
        state[...] = st * jnp.exp(cs_last) + _dot_tn(bm, (xdt * jnp.exp(cs_last - cse)).astype(BF16))
        y = y + d_ref[...] * xs
        y = y * _silu(z_ref[sl, :])
        y = y * lax.rsqrt(jnp.mean(y * y, axis=-1, keepdims=True) + EPS) * ng_ref[...]
        o_ref[sl, :] = y.astype(o_ref.dtype)


def _ssd(proj, dt_proj, cw, cb, dtb, alog, dexp, ng, expand, *, bsz, seq, groups, heads, hdim, nst,
         z_col, xs_col, bm_col, cm_col, rows):
    n_t = seq // rows
    wg = heads * hdim
    width = groups * wg
    row_map = lambda b, g, t: b * n_t + t
    k = cw.shape[0]
    cb2 = cb.reshape(1, -1)

    def col(width_, base):
        return pl.BlockSpec((rows, width_), lambda b, g, t: (row_map(b, g, t), base // width_ + g))

    def par(nrow, width_, base):
        return pl.BlockSpec((nrow, width_), lambda b, g, t: (0, base // width_ + g))

    return pl.pallas_call(
        functools.partial(_ssd_body, heads=heads, hdim=hdim),
        grid=(bsz, groups, n_t),
        in_specs=[col(wg, xs_col), col(nst, bm_col), col(nst, cm_col), col(LANE, 0), col(wg, z_col),
                  par(k, wg, 0), par(k, nst, width), par(k, nst, width + groups * nst),
                  par(1, wg, 0), par(1, nst, width), par(1, nst, width + groups * nst),
                  par(1, LANE, 0), par(1, LANE, 0), par(1, wg, 0), par(1, wg, 0),
                  pl.BlockSpec((LANE, wg), lambda b, g, t: (0, 0))],
        out_specs=pl.BlockSpec((rows, wg), lambda b, g, t: (row_map(b, g, t), g)),
        out_shape=jax.ShapeDtypeStruct((bsz * seq, width), BF16),
        scratch_shapes=[pltpu.VMEM((rows + SUBLANE, wg), F32),
                        pltpu.VMEM((rows + SUBLANE, nst), F32),
                        pltpu.VMEM((rows + SUBLANE, nst), F32),
                        pltpu.VMEM((nst, wg), F32)],
        compiler_params=_cparams("parallel", "parallel", "arbitrary"),
        name="ssd",
    )(proj, proj, proj, dt_proj, proj, cw, cw, cw, cb2, cb2, cb2, dtb, alog, dexp, ng, expand)


def _gla_body(q_ref, k_ref, v_ref, r_ref, gl_ref, wg_ref, bg_ref, ng_ref, o_ref, state, *, q_scale):
    rows, dk = q_ref.shape

    @pl.when(pl.program_id(2) == 0)
    def _():
        state[...] = jnp.zeros(state.shape, F32)

    la_all = jax.nn.log_sigmoid(_dot(gl_ref[...].astype(BF16), wg_ref[...]) + bg_ref[...]) / GATE_TAU
    tri = _tri(CHUNK)
    lane = lax.broadcasted_iota(jnp.int32, (HALF, CHUNK), 1)
    srow = lax.broadcasted_iota(jnp.int32, (HALF, CHUNK), 0)

    for c in range(rows // CHUNK):
        sl = slice(CHUNK * c, CHUNK * (c + 1))
        q = q_ref[sl, :] * q_scale
        k = k_ref[sl, :]
        v = v_ref[sl, :].astype(BF16)
        b = _sel_dot_left(tri, la_all[sl])
        b_last = b[CHUNK - 1:CHUNK, :]
        st = state[...]
        o = _dot_nt((q * jnp.exp(b)).astype(BF16), st.astype(BF16))
        state[...] = st * jnp.exp(b_last) + _dot_tn(v, (k * jnp.exp(b_last - b)).astype(BF16))

        att_rows = []
        for blk in range(CHUNK // SUB):
            r0 = SUB * blk
            qi, ki, bi = q[r0:r0 + SUB], k[r0:r0 + SUB], b[r0:r0 + SUB]
            halves = []
            for h0 in (0, HALF):
                qh, kh, bh = qi[h0:h0 + HALF], ki[h0:h0 + HALF], bi[h0:h0 + HALF]
                att = jnp.zeros((HALF, CHUNK), F32)
                for j in range(HALF):
                    e = jnp.exp(jnp.minimum(bh - bh[j:j + 1, :], 0.0))
                    s = jnp.sum(qh * kh[j:j + 1, :] * e, axis=-1, keepdims=True)
                    att = jnp.where(lane == r0 + h0 + j, s, att)
                halves.append(jnp.where(lane - (r0 + h0) <= srow, att, 0.0))
            bmid = bi[HALF - 1:HALF, :]
            q_lo = (qi[HALF:] * jnp.exp(bi[HALF:] - bmid)).astype(BF16)
            k_mid = jnp.concatenate([*([jnp.zeros((r0, dk), F32)] if r0 else []), ki[:HALF] * jnp.exp(bmid - bi[:HALF]),
                                     jnp.zeros((CHUNK - r0 - HALF, dk), F32)], axis=0).astype(BF16)
            blk_att = jnp.concatenate([halves[0], halves[1] + _dot_nt(q_lo, k_mid)], axis=0)
            if blk > 0:
                bref = b[r0 - 1:r0, :]
                qt = (qi * jnp.exp(bi - bref)).astype(BF16)
                kt = jnp.concatenate([k[:r0] * jnp.exp(bref - b[:r0]),
                                      jnp.zeros((CHUNK - r0, dk), F32)], axis=0).astype(BF16)
                blk_att = blk_att + _dot_nt(qt, kt)
            att_rows.append(blk_att)
        att = jnp.concatenate(att_rows, axis=0)
        o = o + _dot(att.astype(BF16), v)
        o = o * lax.rsqrt(jnp.mean(o * o, axis=-1, keepdims=True) + EPS)
        o = o * ng_ref[...] * _silu(r_ref[sl, :])
        o_ref[sl, :] = o.astype(o_ref.dtype)


def _gla(proj, gl_proj, wg, bg, ng, *, bsz, seq, heads, dk, dv, q_col, k_col, v_col, r_col, rows):
    n_t = seq // rows
    row_map = lambda b, h, t: b * n_t + t

    def col(width_, base):
        return pl.BlockSpec((rows, width_), lambda b, h, t: (row_map(b, h, t), base // width_ + h))

    return pl.pallas_call(
        functools.partial(_gla_body, q_scale=dk ** -0.5),
        grid=(bsz, heads, n_t),
        in_specs=[col(dk, q_col), col(dk, k_col), col(dv, v_col), col(dv, r_col),
                  pl.BlockSpec((rows, LANE), lambda b, h, t: (row_map(b, h, t), 0)),
                  pl.BlockSpec((LANE, dk), lambda b, h, t: (0, h)),
                  pl.BlockSpec((1, dk), lambda b, h, t: (0, h)),
                  pl.BlockSpec((1, dv), lambda b, h, t: (0, h))],
        out_specs=pl.BlockSpec((rows, dv), lambda b, h, t: (row_map(b, h, t), h)),
        out_shape=jax.ShapeDtypeStruct((bsz * seq, heads * dv), BF16),
        scratch_shapes=[pltpu.VMEM((dv, dk), F32)],
        compiler_params=_cparams("parallel", "parallel", "arbitrary"),
        name="gla",
    )(proj, proj, proj, proj, gl_proj, wg, bg, ng)


def _xattn_body(q_ref, k_ref, v_ref, o_ref, *, heads, scale):
    dh = q_ref.shape[1] // heads
    for h in range(heads):
        sl = slice(dh * h, dh * (h + 1))
        s = _dot_nt(q_ref[:, sl], k_ref[:, sl]) * scale
        s = s - jnp.max(s, axis=-1, keepdims=True)
        p = jnp.exp(s)
        p = p / jnp.sum(p, axis=-1, keepdims=True)
        o_ref[:, sl] = _dot(p.astype(BF16), v_ref[:, sl]).astype(o_ref.dtype)


def _xattn(q, kv, *, seq, mem_len, tm):
    m, d = q.shape
    per_b = seq // tm
    return pl.pallas_call(
        functools.partial(_xattn_body, heads=XA_HEADS, scale=(d // XA_HEADS) ** -0.5),
        grid=(m // tm,),
        in_specs=[pl.BlockSpec((tm, d), lambda i: (i, 0)),
                  pl.BlockSpec((mem_len, d), lambda i: (i // per_b, 0)),
                  pl.BlockSpec((mem_len, d), lambda i: (i // per_b, 1))],
        out_specs=pl.BlockSpec((tm, d), lambda i: (i, 0)),
        out_shape=jax.ShapeDtypeStruct((m, d), BF16),
        compiler_params=_cparams("parallel"),
        name="xattn",
    )(q, kv, kv)


def _ffn_body(x_ref, g_ref, wv_ref, wg_ref, cwv_ref, cwg_ref, wo_ref, fg_ref,
              o_ref, xn_ref, act_prev, act_next, carry, *bufs, rows, n_i, n_j, tiles_per_seq, final_norm, sub):
    tm = x_ref.shape[0]
    tf = wv_ref.shape[1]
    n_sub = tf // sub
    k = cwv_ref.shape[0] - 1
    s = pl.program_id(0)
    cur = jnp.minimum(s, n_i * n_j - 1)
    i = cur // n_j
    j = cur % n_j
    jp = jnp.maximum(s - 1, 0) % n_j

    def row_blocks(fn):
        def blk(r, c):
            fn(pl.ds(pl.multiple_of(r * rows, rows), rows))
            return c
        lax.fori_loop(0, tm // rows, blk, 0)

    @pl.when(s == 0)
    def _():
        act_next[...] = jnp.zeros(act_next.shape, BF16)
        o_ref[...] = jnp.zeros(o_ref.shape, F32)

    @pl.when(j == 0)
    def _():
        def norm(sl):
            xn_ref[sl, :] = _rms_rows(x_ref[sl, :], g_ref[...]).astype(BF16)
        row_blocks(norm)

    @pl.when((jp == 0) & (s > 0))
    def _():
        o_ref[...] = x_ref[...]

    @pl.when(i % tiles_per_seq == 0)
    def _():
        carry[j] = jnp.zeros(carry.shape[1:], F32)

    act_prev[...] = act_next[...]
    xn = xn_ref[...]
    for t in range(2 * n_sub):
        cs = slice(sub * (t % n_sub), sub * (t % n_sub + 1))
        bufs[t][pl.ds(0, SUBLANE), :] = carry[j, pl.ds(SUBLANE * (t // n_sub), SUBLANE), cs]

    def conv(w_ref, buf, cw_ref, cs):
        buf[pl.ds(SUBLANE, tm), :] = _dot(xn, w_ref[:, cs])
        y = cw_ref[k:k + 1, cs] + cw_ref[k - 1:k, cs] * buf[pl.ds(SUBLANE, tm), :]
        for d in range(1, k):
            y = y + cw_ref[k - 1 - d:k - d, cs] * buf[pl.ds(SUBLANE - d, tm), :]
        return y

    for t in range(n_sub):
        cs = slice(sub * t, sub * (t + 1))
        val = conv(wv_ref, bufs[t], cwv_ref, cs)
        gate = conv(wg_ref, bufs[n_sub + t], cwg_ref, cs)
        act_next[:, cs] = (val * jax.nn.gelu(gate)).astype(BF16)
    o_ref[...] += _dot(act_prev[...], wo_ref[...])

    for t in range(2 * n_sub):
        cs = slice(sub * (t % n_sub), sub * (t % n_sub + 1))
        carry[j, pl.ds(SUBLANE * (t // n_sub), SUBLANE), cs] = bufs[t][pl.ds(tm, SUBLANE), :]

    if final_norm:
        @pl.when((jp == n_j - 1) & (s > 0))
        def _():
            def norm(sl):
                o_ref[sl, :] = _rms_rows(o_ref[sl, :], fg_ref[...])
            row_blocks(norm)


def _ffn(x, g, w_in, cwb, w_out, fg, *, layer, seq, tm, tf, final_norm, sub=256):
    m, d = x.shape
    d_ff = w_out.shape[1]
    n_i = m // tm
    n_j = d_ff // tf
    assert n_j >= 2
    kb = cwb.shape[1]
    last = n_i * n_j - 1
    ci = lambda s: jnp.minimum(s, last) // n_j
    cj = lambda s: jnp.minimum(s, last) % n_j
    pi = lambda s: jnp.maximum(s - 1, 0) // n_j
    pj = lambda s: jnp.maximum(s - 1, 0) % n_j
    return pl.pallas_call(
        functools.partial(_ffn_body, rows=min(tm, 256), n_i=n_i, n_j=n_j, tiles_per_seq=seq // tm,
                          final_norm=final_norm, sub=sub),
        grid=(n_i * n_j + 1,),
        in_specs=[pl.BlockSpec((tm, d), lambda s: (ci(s), 0)),
                  pl.BlockSpec((1, d), lambda s: (0, 0)),
                  pl.BlockSpec((None, d, tf), lambda s: (layer, 0, cj(s))),
                  pl.BlockSpec((None, d, tf), lambda s: (layer, 0, n_j + cj(s))),
                  pl.BlockSpec((None, kb, tf), lambda s: (layer, 0, cj(s))),
                  pl.BlockSpec((None, kb, tf), lambda s: (layer, 0, n_j + cj(s))),
                  pl.BlockSpec((None, tf, d), lambda s: (layer, pj(s), 0)),
                  pl.BlockSpec((1, d), lambda s: (0, 0))],
        out_specs=pl.BlockSpec((tm, d), lambda s: (pi(s), 0)),
        out_shape=jax.ShapeDtypeStruct((m, d), F32),
        scratch_shapes=[pltpu.VMEM((tm, d), BF16),
                        pltpu.VMEM((tm, tf), BF16),
                        pltpu.VMEM((tm, tf), BF16),
                        pltpu.VMEM((n_j, 2 * SUBLANE, tf), F32)]
                       + [pltpu.VMEM((tm + SUBLANE, sub), F32) for _ in range(2 * tf // sub)],
        compiler_params=_cparams("arbitrary"),
        name="conv_ffn",
    )(x, g.reshape(1, d), w_in, w_in, cwb, cwb, w_out, fg.reshape(1, d))


def _pad_cols(w, n):
    return jnp.pad(w, ((0, 0), (0, n - w.shape[1])))


def _group_lanes(v, groups, heads):
    lead = v.shape[:-1]
    v = v.reshape(lead + (groups, heads))
    v = jnp.pad(v, [(0, 0)] * len(lead) + [(0, 0), (0, LANE - heads)])
    return v.reshape(lead + (groups * LANE,))


def kernel(x, mem, norm_mix_g, norm_cross_g, norm_ffn_g, ab_w_in, lru_conv_w, lru_conv_b, lru_w_a, lru_b_a, lru_w_x, lru_b_x, lru_lambda, ssd_conv_w, ssd_conv_b, ssd_dt_bias, ssd_a_log, ssd_d, ssd_norm_g, ab_w_out, gla_w_in, gla_w_gate_up, gla_b_gate, gla_norm_g, gla_w_out, mem_norm_g, xa_w_q, xa_w_kv, xa_w_o, ffn_w_in, ffn_conv_w, ffn_conv_b, ffn_w_out, final_norm_g):
    bsz, seq, d = x.shape
    mem_len = mem.shape[1]
    depth = norm_mix_g.shape[0]
    m = bsz * seq
    assert seq % 1024 == 0 and d % 512 == 0

    w_a = lru_lambda.shape[1]
    h_b = ssd_dt_bias.shape[1]
    w_b = ssd_norm_g.shape[1]
    p_b = w_b // h_b
    conv_dim = ssd_conv_w.shape[2]
    g_b = 4
    n_b = (conv_dim - w_b) // (2 * g_b)
    r_b = h_b // g_b
    assert ab_w_in.shape[2] == 2 * w_a + w_b + conv_dim + h_b and r_b * p_b == 512 and n_b == LANE
    dk_c = gla_b_gate.shape[1]
    dv_c = gla_norm_g.shape[1]
    rank = gla_w_gate_up.shape[1]
    h_c = 4
    assert gla_w_in.shape[2] == 2 * dk_c + 2 * dv_c + rank and rank <= LANE

    x2 = x.reshape(m, d)
    kv_in = mem.reshape(bsz * mem_len, d)
    expand = (lax.broadcasted_iota(jnp.int32, (LANE, r_b * p_b), 1) // p_b
              == lax.broadcasted_iota(jnp.int32, (LANE, r_b * p_b), 0)).astype(BF16)
    ab_w_in_bf, ab_w_out_bf = ab_w_in.astype(BF16), ab_w_out.astype(BF16)
    gla_w_in_bf, gla_w_out_bf = gla_w_in.astype(BF16), gla_w_out.astype(BF16)
    xa_w_q_bf, xa_w_kv_bf, xa_w_o_bf = xa_w_q.astype(BF16), xa_w_kv.astype(BF16), xa_w_o.astype(BF16)
    ffn_w_in_bf, ffn_w_out_bf = ffn_w_in.astype(BF16), ffn_w_out.astype(BF16)
    ffn_cwb = jnp.concatenate([ffn_conv_w, ffn_conv_b[:, None, :]], axis=1)

    for layer in range(depth):
        j = layer // 2
        if layer % 2 == 0:
            main = 2 * w_a + w_b + conv_dim
            assert main % 1024 == 0
            w_tail = _group_lanes(ab_w_in_bf[j][:, main:], g_b, r_b)
            proj, dt_proj = _norm_matmul(x2, norm_mix_g[layer], ab_w_in_bf, layer=j, tm=1024, tn=1024,
                                         out_dtype=F32, w_tail=w_tail)
            y_a = _rglru(proj, lru_conv_w[j], lru_conv_b[j], lru_w_a[j].astype(BF16), lru_b_a[j],
                         lru_w_x[j].astype(BF16), lru_b_x[j], lru_lambda[j],
                         bsz=bsz, seq=seq, width=w_a, gate_col=0, x_col=w_a, rows=512, tc=512)
            y_b = _ssd(proj, dt_proj, ssd_conv_w[j], ssd_conv_b[j],
                       _group_lanes(ssd_dt_bias[j], g_b, r_b).reshape(1, -1),
                       _group_lanes(ssd_a_log[j], g_b, r_b).reshape(1, -1),
                       jnp.repeat(ssd_d[j], p_b).reshape(1, -1), ssd_norm_g[j].reshape(1, -1), expand,
                       bsz=bsz, seq=seq, groups=g_b, heads=r_b, hdim=p_b, nst=n_b,
                       z_col=2 * w_a, xs_col=2 * w_a + w_b, bm_col=2 * w_a + 2 * w_b,
                       cm_col=2 * w_a + 2 * w_b + g_b * n_b, rows=256)
            x2 = _matmul_res([y_a, y_b], [(ab_w_out_bf, j, 0), (ab_w_out_bf, j, 1)], x2, tm=512, tn=1024)
        else:
            main = 2 * dk_c + 2 * dv_c
            assert main % 1024 == 0
            proj, gl_proj = _norm_matmul(x2, norm_mix_g[layer], gla_w_in_bf, layer=j, tm=1024, tn=1024, out_dtype=F32,
                                         w_tail=_pad_cols(gla_w_in_bf[j][:, main:], LANE))
            wg = jnp.pad(gla_w_gate_up[j], ((0, LANE - rank), (0, 0))).astype(BF16)
            o = _gla(proj, gl_proj, wg, gla_b_gate[j].reshape(1, -1), gla_norm_g[j].reshape(1, -1),
                     bsz=bsz, seq=seq, heads=h_c, dk=dk_c // h_c, dv=dv_c // h_c,
                     q_col=0, k_col=dk_c, v_col=2 * dk_c, r_col=2 * dk_c + dv_c, rows=1024)
            x2 = _matmul_res([o], [(gla_w_out_bf, j, 0)], x2, tm=512, tn=d)

        q = _norm_matmul(x2, norm_cross_g[layer], xa_w_q_bf, layer=layer, tm=1024, tn=1024, out_dtype=BF16)
        kv = _norm_matmul(kv_in, mem_norm_g, xa_w_kv_bf, layer=layer, tm=bsz * mem_len, tn=1024, out_dtype=BF16)
        att = _xattn(q, kv, seq=seq, mem_len=mem_len, tm=512)
        x2 = _matmul_res([att], [(xa_w_o_bf, layer, 0)], x2, tm=512, tn=d)

        x2 = _ffn(x2, norm_ffn_g[layer], ffn_w_in_bf, ffn_cwb, ffn_w_out_bf, final_norm_g, layer=layer,
                  seq=seq, tm=512, tf=1024, final_norm=(layer == depth - 1))
    return x2.reshape(bsz, seq, d)
```

```python
import functools
import math

import jax
import jax.numpy as jnp
from jax import lax
from jax.experimental import pallas as pl
from jax.experimental.pallas import tpu as pltpu

F32 = jnp.float32
BF16 = jnp.bfloat16

EPS = 1e-6
CHUNK = 64
SUB = 16
SSD_CHUNK = 128
HALF = 8
LRU_C = 8.0
GATE_TAU = 16.0
XA_HEADS = 4
LANE = 128
SUBLANE = 8
VMEM_LIMIT = 56 * 1024 * 1024


def _cparams(*sem):
    return pltpu.CompilerParams(dimension_semantics=sem, vmem_limit_bytes=VMEM_LIMIT)


def _dot(a, b):
    return jnp.dot(a, b, preferred_element_type=F32)


def _dot_nt(a, b):
    return lax.dot_general(a, b, (((1,), (1,)), ((), ())), preferred_element_type=F32)


def _dot_tn(a, b):
    return lax.dot_general(a, b, (((0,), (0,)), ((), ())), preferred_element_type=F32)


def _split3(x):
    hi = x.astype(BF16)
    r1 = x - hi.astype(F32)
    mid = r1.astype(BF16)
    lo = (r1 - mid.astype(F32)).astype(BF16)
    return hi, mid, lo


def _sel_dot_left(sel, x):
    hi, mid, lo = _split3(x)
    return _dot(sel, hi) + _dot(sel, mid) + _dot(sel, lo)


def _sel_dot_right(x, sel):
    hi, mid, lo = _split3(x)
    return _dot(hi, sel) + _dot(mid, sel) + _dot(lo, sel)


def _tri(n):
    i = lax.broadcasted_iota(jnp.int32, (n, n), 0)
    j = lax.broadcasted_iota(jnp.int32, (n, n), 1)
    return jnp.where(j <= i, 1.0, 0.0).astype(BF16)


def _sigmoid(x):
    return 0.5 * jnp.tanh(0.5 * x) + 0.5


def _silu(x):
    return x * _sigmoid(x)


def _rms_rows(x, g):
    ms = jnp.mean(x * x, axis=-1, keepdims=True)
    return x * lax.rsqrt(ms + EPS) * g


def _conv_tile(in_ref, buf, cw, cb, rows):
    k = cw.shape[0]
    x = in_ref[...]
    buf[pl.ds(SUBLANE, rows), :] = x
    y = cb + cw[k - 1:k, :] * x
    for s in range(1, k):
        y = y + cw[k - 1 - s:k - s, :] * buf[pl.ds(SUBLANE - s, rows), :]
    buf[pl.ds(0, SUBLANE), :] = buf[pl.ds(rows, SUBLANE), :]
    return y


def _norm_matmul_body(x_ref, g_ref, w_ref, *rest, rows, has_tail):
    if has_tail:
        wt_ref, o_ref, ot_ref, xn_ref = rest
    else:
        o_ref, xn_ref = rest
    tm = x_ref.shape[0]

    @pl.when(pl.program_id(1) == 0)
    def _():
        def blk(r, c):
            sl = pl.ds(pl.multiple_of(r * rows, rows), rows)
            xn_ref[sl, :] = _rms_rows(x_ref[sl, :], g_ref[...]).astype(BF16)
            return c
        lax.fori_loop(0, tm // rows, blk, 0)
        if has_tail:
            ot_ref[...] = _dot(xn_ref[...], wt_ref[...]).astype(ot_ref.dtype)

    o_ref[...] = _dot(xn_ref[...], w_ref[...]).astype(o_ref.dtype)


def _norm_matmul(x, g, w, *, layer, tm, tn, out_dtype, w_tail=None):
    m, d = x.shape
    n_main = w.shape[2] // tn
    assert w_tail is not None or w.shape[2] == n_main * tn
    in_specs = [pl.BlockSpec((tm, d), lambda i, j: (i, 0)),
                pl.BlockSpec((1, d), lambda i, j: (0, 0)),
                pl.BlockSpec((None, d, tn), lambda i, j: (layer, 0, j))]
    out_specs = pl.BlockSpec((tm, tn), lambda i, j: (i, j))
    out_shape = jax.ShapeDtypeStruct((m, n_main * tn), out_dtype)
    args = [x, g.reshape(1, d), w]
    if w_tail is not None:
        n_t = w_tail.shape[1]
        in_specs.append(pl.BlockSpec((d, n_t), lambda i, j: (0, 0)))
        out_specs = [out_specs, pl.BlockSpec((tm, n_t), lambda i, j: (i, 0))]
        out_shape = [out_shape, jax.ShapeDtypeStruct((m, n_t), out_dtype)]
        args.append(w_tail)
    return pl.pallas_call(
        functools.partial(_norm_matmul_body, rows=min(tm, 256), has_tail=w_tail is not None),
        grid=(m // tm, n_main),
        in_specs=in_specs,
        out_specs=out_specs,
        out_shape=out_shape,
        scratch_shapes=[pltpu.VMEM((tm, d), BF16)],
        compiler_params=_cparams("parallel", "arbitrary"),
        name="norm_matmul",
    )(*args)


def _matmul_res_body(*refs, n_lhs):
    a_refs = refs[:n_lhs]
    w_refs = refs[n_lhs:2 * n_lhs]
    res_ref = refs[2 * n_lhs]
    o_ref = refs[2 * n_lhs + 1]
    acc = res_ref[...]
    for a_ref, w_ref in zip(a_refs, w_refs):
        acc = acc + _dot(a_ref[...], w_ref[...])
    o_ref[...] = acc


def _matmul_res(lhs, ws, res, *, tm, tn):
    m, n = res.shape
    n_lhs = len(lhs)
    in_specs = [pl.BlockSpec((tm, a.shape[1]), lambda j, i: (i, 0)) for a in lhs]
    for a, (w, layer, rb) in zip(lhs, ws):
        in_specs.append(pl.BlockSpec((None, a.shape[1], tn), lambda j, i, layer=layer, rb=rb: (layer, rb, j)))
    in_specs += [pl.BlockSpec((tm, tn), lambda j, i: (i, j))]
    return pl.pallas_call(
        functools.partial(_matmul_res_body, n_lhs=n_lhs),
        grid=(n // tn, m // tm),
        in_specs=in_specs,
        out_specs=pl.BlockSpec((tm, tn), lambda j, i: (i, j)),
        out_shape=jax.ShapeDtypeStruct((m, n), F32),
        compiler_params=_cparams("parallel", "arbitrary"),
        name="matmul_res",
    )(*lhs, *[w for w, _, _ in ws], res)


def _rglru_body(gate_ref, xa_ref, cw_ref, cb_ref, wa_ref, ba_ref, wx_ref, bx_ref, lam_ref,
                o_ref, xbuf, a_s, u_s, hc):
    rows, tc = xa_ref.shape

    @pl.when(pl.program_id(2) == 0)
    def _():
        xbuf[pl.ds(0, SUBLANE), :] = jnp.zeros((SUBLANE, tc), F32)
        hc[...] = jnp.zeros((SUBLANE, tc), F32)

    xc = _conv_tile(xa_ref, xbuf, cw_ref[...], cb_ref[...], rows)
    sp = jax.nn.softplus(-lam_ref[...])
    for h in range(tc // LANE):
        sl = slice(LANE * h, LANE * (h + 1))
        xh = xc[:, sl]
        xb = xh.astype(BF16)
        r = _sigmoid(_dot(xb, wa_ref[h]) + ba_ref[:, sl])
        i = _sigmoid(_dot(xb, wx_ref[h]) + bx_ref[:, sl])
        log_a = -LRU_C * r * sp[:, sl]
        th = jnp.tanh(log_a)
        a_s[:, sl] = jnp.exp(log_a)
        u_s[:, sl] = jnp.sqrt(-2.0 * th / (1.0 - th)) * (i * xh)

    row = lax.broadcasted_iota(jnp.int32, (SUBLANE, tc), 0)

    def step(g, h_prev):
        sl = pl.ds(pl.multiple_of(g * SUBLANE, SUBLANE), SUBLANE)
        a = a_s[sl, :]
        u = u_s[sl, :]
        for s in (1, 2, 4):
            keep = row >= s
            a_sh = jnp.where(keep, pltpu.roll(a, s, 0), 1.0)
            u_sh = jnp.where(keep, pltpu.roll(u, s, 0), 0.0)
            u = a * u_sh + u
            a = a * a_sh
        h = a * h_prev + u
        u_s[sl, :] = h
        return jnp.broadcast_to(h[SUBLANE - 1:SUBLANE, :], (SUBLANE, tc))

    hc[...] = lax.fori_loop(0, rows // SUBLANE, step, hc[...])
    o_ref[...] = (jax.nn.gelu(gate_ref[...]) * u_s[...]).astype(o_ref.dtype)


def _rglru(proj, cw, cb, wa, ba, wx, bx, lam, *, bsz, seq, width, gate_col, x_col, rows, tc):
    n_t = seq // rows
    n_c = width // tc
    hpt = tc // LANE
    row_map = lambda b, c, t: b * n_t + t
    vec = lambda: pl.BlockSpec((1, tc), lambda b, c, t: (0, c))
    return pl.pallas_call(
        _rglru_body,
        grid=(bsz, n_c, n_t),
        in_specs=[pl.BlockSpec((rows, tc), lambda b, c, t: (row_map(b, c, t), gate_col // tc + c)),
                  pl.BlockSpec((rows, tc), lambda b, c, t: (row_map(b, c, t), x_col // tc + c)),
                  pl.BlockSpec((cw.shape[0], tc), lambda b, c, t: (0, c)),
                  vec(),
                  pl.BlockSpec((hpt, LANE, LANE), lambda b, c, t: (c, 0, 0)),
                  vec(),
                  pl.BlockSpec((hpt, LANE, LANE), lambda b, c, t: (c, 0, 0)),
                  vec(), vec()],
        out_specs=pl.BlockSpec((rows, tc), lambda b, c, t: (row_map(b, c, t), c)),
        out_shape=jax.ShapeDtypeStruct((bsz * seq, width), BF16),
        scratch_shapes=[pltpu.VMEM((rows + SUBLANE, tc), F32),
                        pltpu.VMEM((rows, tc), F32),
                        pltpu.VMEM((rows, tc), F32),
                        pltpu.VMEM((SUBLANE, tc), F32)],
        compiler_params=_cparams("parallel", "parallel", "arbitrary"),
        name="rglru",
    )(proj, proj, cw, cb.reshape(1, -1), wa, ba.reshape(1, -1), wx, bx.reshape(1, -1), lam.reshape(1, -1))


def _ssd_body(xs_ref, bm_ref, cm_ref, dt_ref, z_ref, cwx_ref, cwb_ref, cwc_ref, cbx_ref, cbb_ref, cbc_ref,
              dtb_ref, alog_ref, d_ref, ng_ref, e_ref, o_ref, bufx, bufb, bufc, state, *, heads, hdim):
    rows, wg = xs_ref.shape
    nst = bm_ref.shape[1]

    @pl.when(pl.program_id(2) == 0)
    def _():
        bufx[pl.ds(0, SUBLANE), :] = jnp.zeros((SUBLANE, wg), F32)
        bufb[pl.ds(0, SUBLANE), :] = jnp.zeros((SUBLANE, nst), F32)
        bufc[pl.ds(0, SUBLANE), :] = jnp.zeros((SUBLANE, nst), F32)
        state[...] = jnp.zeros(state.shape, F32)

    xs_all = _silu(_conv_tile(xs_ref, bufx, cwx_ref[...], cbx_ref[...], rows))
    bm_all = _silu(_conv_tile(bm_ref, bufb, cwb_ref[...], cbb_ref[...], rows)).astype(BF16)
    cm_all = _silu(_conv_tile(cm_ref, bufc, cwc_ref[...], cbc_ref[...], rows)).astype(BF16)
    dt_all = jax.nn.softplus(dt_ref[...] + dtb_ref[...])
    dta_all = dt_all * (-jnp.exp(alog_ref[...]))
    expand = e_ref[...]
    dte_all = _sel_dot_right(dt_all, expand)
    ri = lax.broadcasted_iota(jnp.int32, (rows, rows), 0)
    rj = lax.broadcasted_iota(jnp.int32, (rows, rows), 1)
    tri = jnp.where((rj <= ri) & (rj // SSD_CHUNK == ri // SSD_CHUNK), 1.0, 0.0).astype(BF16)
    cs_all = _sel_dot_left(tri, dta_all)
    cse_all = _sel_dot_right(cs_all, expand)
    ii = lax.broadcasted_iota(jnp.int32, (SSD_CHUNK, SSD_CHUNK), 0)
    jj = lax.broadcasted_iota(jnp.int32, (SSD_CHUNK, SSD_CHUNK), 1)
    causal = jj <= ii

    for c in range(rows // SSD_CHUNK):
        sl = slice(SSD_CHUNK * c, SSD_CHUNK * (c + 1))
        xs = xs_all[sl]
        bm = bm_all[sl]
        cm = cm_all[sl]
        cs = cs_all[sl]
        cs_t = cs.T
        cse = cse_all[sl]
        cs_last = cse[SSD_CHUNK - 1:SSD_CHUNK, :]
        xdt = xs * dte_all[sl]
        cb = _dot_nt(cm, bm)
        parts = []
        for r in range(heads):
            seg = cs[:, r:r + 1] - cs_t[r:r + 1, :]
            lmat = jnp.exp(jnp.where(causal, seg, -jnp.inf))
            parts.append(_dot((cb * lmat).astype(BF16), xdt[:, r * hdim:(r + 1) * hdim].astype(BF16)))
        y = jnp.concatenate(parts, axis=1)
        st = state[...]
        y = y + _dot(cm, st.astype(BF16)) * jnp.exp(cse)
        state[...] = st * jnp.exp(cs_last) + _dot_tn(bm, (xdt * jnp.exp(cs_last - cse)).astype(BF16))
        y = y + d_ref[...] * xs
        y = y * _silu(z_ref[sl, :])
        y = y * lax.rsqrt(jnp.mean(y * y, axis=-1, keepdims=True) + EPS) * ng_ref[...]
        o_ref[sl, :] = y.astype(o_ref.dtype)


def _ssd(proj, dt_proj, cw, cb, dtb, alog, dexp, ng, expand, *, bsz, seq, groups, heads, hdim, nst,
         z_col, xs_col, bm_col, cm_col, rows):
    n_t = seq // rows
    wg = heads * hdim
    width = groups * wg
    row_map = lambda b, g, t: b * n_t + t
    k = cw.shape[0]
    cb2 = cb.reshape(1, -1)

    def col(width_, base):
        return pl.BlockSpec((rows, width_), lambda b, g, t: (row_map(b, g, t), base // width_ + g))

    def par(nrow, width_, base):
        return pl.BlockSpec((nrow, width_), lambda b, g, t: (0, base // width_ + g))

    return pl.pallas_call(
        functools.partial(_ssd_body, heads=heads, hdim=hdim),
        grid=(bsz, groups, n_t),
        in_specs=[col(wg, xs_col), col(nst, bm_col), col(nst, cm_col), col(LANE, 0), col(wg, z_col),
                  par(k, wg, 0), par(k, nst, width), par(k, nst, width + groups * nst),
                  par(1, wg, 0), par(1, nst, width), par(1, nst, width + groups * nst),
                  par(1, LANE, 0), par(1, LANE, 0), par(1, wg, 0), par(1, wg, 0),
                  pl.BlockSpec((LANE, wg), lambda b, g, t: (0, 0))],
        out_specs=pl.BlockSpec((rows, wg), lambda b, g, t: (row_map(b, g, t), g)),
        out_shape=jax.ShapeDtypeStruct((bsz * seq, width), BF16),
        scratch_shapes=[pltpu.VMEM((rows + SUBLANE, wg), F32),
                        pltpu.VMEM((rows + SUBLANE, nst), F32),
                        pltpu.VMEM((rows + SUBLANE, nst), F32),
                        pltpu.VMEM((nst, wg), F32)],
        compiler_params=_cparams("parallel", "parallel", "arbitrary"),
        name="ssd",
    )(proj, proj, proj, dt_proj, proj, cw, cw, cw, cb2, cb2, cb2, dtb, alog, dexp, ng, expand)


def _gla_body(q_ref, k_ref, v_ref, r_ref, gl_ref, wg_ref, bg_ref, ng_ref, o_ref, state, *, q_scale):
    rows, dk = q_ref.shape

    @pl.when(pl.program_id(2) == 0)
    def _():
        state[...] = jnp.zeros(state.shape, F32)

    la_all = jax.nn.log_sigmoid(_dot(gl_ref[...].astype(BF16), wg_ref[...]) + bg_ref[...]) / GATE_TAU
    tri = _tri(CHUNK)
    lane = lax.broadcasted_iota(jnp.int32, (HALF, CHUNK), 1)
    srow = lax.broadcasted_iota(jnp.int32, (HALF, CHUNK), 0)

    for c in range(rows // CHUNK):
        sl = slice(CHUNK * c, CHUNK * (c + 1))
        q = q_ref[sl, :] * q_scale
        k = k_ref[sl, :]
        v = v_ref[sl, :].astype(BF16)
        b = _sel_dot_left(tri, la_all[sl])
        b_last = b[CHUNK - 1:CHUNK, :]
        st = state[...]
        o = _dot_nt((q * jnp.exp(b)).astype(BF16), st.astype(BF16))
        state[...] = st * jnp.exp(b_last) + _dot_tn(v, (k * jnp.exp(b_last - b)).astype(BF16))

        att_rows = []
        for blk in range(CHUNK // SUB):
            r0 = SUB * blk
            qi, ki, bi = q[r0:r0 + SUB], k[r0:r0 + SUB], b[r0:r0 + SUB]
            halves = []
            for h0 in (0, HALF):
                qh, kh, bh = qi[h0:h0 + HALF], ki[h0:h0 + HALF], bi[h0:h0 + HALF]
                att = jnp.zeros((HALF, CHUNK), F32)
                for j in range(HALF):
                    e = jnp.exp(jnp.minimum(bh - bh[j:j + 1, :], 0.0))
                    s = jnp.sum(qh * kh[j:j + 1, :] * e, axis=-1, keepdims=True)
                    att = jnp.where(lane == r0 + h0 + j, s, att)
                halves.append(jnp.where(lane - (r0 + h0) <= srow, att, 0.0))
            bmid = bi[HALF - 1:HALF, :]
            q_lo = (qi[HALF:] * jnp.exp(bi[HALF:] - bmid)).astype(BF16)
            k_mid = jnp.concatenate([*([jnp.zeros((r0, dk), F32)] if r0 else []), ki[:HALF] * jnp.exp(bmid - bi[:HALF]),
                                     jnp.zeros((CHUNK - r0 - HALF, dk), F32)], axis=0).astype(BF16)
            blk_att = jnp.concatenate([halves[0], halves[1] + _dot_nt(q_lo, k_mid)], axis=0)
            if blk > 0:
                bref = b[r0 - 1:r0, :]
                qt = (qi * jnp.exp(bi - bref)).astype(BF16)
                kt = jnp.concatenate([k[:r0] * jnp.exp(bref - b[:r0]),
                                      jnp.zeros((CHUNK - r0, dk), F32)], axis=0).astype(BF16)
                blk_att = blk_att + _dot_nt(qt, kt)
            att_rows.append(blk_att)
        att = jnp.concatenate(att_rows, axis=0)
        o = o + _dot(att.astype(BF16), v)
        o = o * lax.rsqrt(jnp.mean(o * o, axis=-1, keepdims=True) + EPS)
        o = o * ng_ref[...] * _silu(r_ref[sl, :])
        o_ref[sl, :] = o.astype(o_ref.dtype)


def _gla(proj, gl_proj, wg, bg, ng, *, bsz, seq, heads, dk, dv, q_col, k_col, v_col, r_col, rows):
    n_t = seq // rows
    row_map = lambda b, h, t: b * n_t + t

    def col(width_, base):
        return pl.BlockSpec((rows, width_), lambda b, h, t: (row_map(b, h, t), base // width_ + h))

    return pl.pallas_call(
        functools.partial(_gla_body, q_scale=dk ** -0.5),
        grid=(bsz, heads, n_t),
        in_specs=[col(dk, q_col), col(dk, k_col), col(dv, v_col), col(dv, r_col),
                  pl.BlockSpec((rows, LANE), lambda b, h, t: (row_map(b, h, t), 0)),
                  pl.BlockSpec((LANE, dk), lambda b, h, t: (0, h)),
                  pl.BlockSpec((1, dk), lambda b, h, t: (0, h)),
                  pl.BlockSpec((1, dv), lambda b, h, t: (0, h))],
        out_specs=pl.BlockSpec((rows, dv), lambda b, h, t: (row_map(b, h, t), h)),
        out_shape=jax.ShapeDtypeStruct((bsz * seq, heads * dv), BF16),
        scratch_shapes=[pltpu.VMEM((dv, dk), F32)],
        compiler_params=_cparams("parallel", "parallel", "arbitrary"),
        name="gla",
    )(proj, proj, proj, proj, gl_proj, wg, bg, ng)


def _xattn_body(q_ref, k_ref, v_ref, o_ref, *, heads, scale):
    dh = q_ref.shape[1] // heads
    for h in range(heads):
        sl = slice(dh * h, dh * (h + 1))
        s = _dot_nt(q_ref[:, sl], k_ref[:, sl]) * scale
        s = s - jnp.max(s, axis=-1, keepdims=True)
        p = jnp.exp(s)
        p = p / jnp.sum(p, axis=-1, keepdims=True)
        o_ref[:, sl] = _dot(p.astype(BF16), v_ref[:, sl]).astype(o_ref.dtype)


def _xattn(q, kv, *, seq, mem_len, tm):
    m, d = q.shape
    per_b = seq // tm
    return pl.pallas_call(
        functools.partial(_xattn_body, heads=XA_HEADS, scale=(d // XA_HEADS) ** -0.5),
        grid=(m // tm,),
        in_specs=[pl.BlockSpec((tm, d), lambda i: (i, 0)),
                  pl.BlockSpec((mem_len, d), lambda i: (i // per_b, 0)),
                  pl.BlockSpec((mem_len, d), lambda i: (i // per_b, 1))],
        out_specs=pl.BlockSpec((tm, d), lambda i: (i, 0)),
        out_shape=jax.ShapeDtypeStruct((m, d), BF16),
        compiler_params=_cparams("parallel"),
        name="xattn",
    )(q, kv, kv)


def _ffn_body(x_ref, g_ref, wv_ref, wg_ref, cwv_ref, cwg_ref, wo_ref, fg_ref,
              o_ref, xn_ref, act_prev, act_next, carry, *bufs, rows, n_i, n_j, tiles_per_seq, final_norm, sub):
    tm = x_ref.shape[0]
    tf = wv_ref.shape[1]
    n_sub = tf // sub
    k = cwv_ref.shape[0] - 1
    s = pl.program_id(0)
    cur = jnp.minimum(s, n_i * n_j - 1)
    i = cur // n_j
    j = cur % n_j
    jp = jnp.maximum(s - 1, 0) % n_j

    def row_blocks(fn):
        def blk(r, c):
            fn(pl.ds(pl.multiple_of(r * rows, rows), rows))
            return c
        lax.fori_loop(0, tm // rows, blk, 0)

    @pl.when(s == 0)
    def _():
        act_next[...] = jnp.zeros(act_next.shape, BF16)
        o_ref[...] = jnp.zeros(o_ref.shape, F32)

    @pl.when(j == 0)
    def _():
        def norm(sl):
            xn_ref[sl, :] = _rms_rows(x_ref[sl, :], g_ref[...]).astype(BF16)
        row_blocks(norm)

    @pl.when((jp == 0) & (s > 0))
    def _():
        o_ref[...] = x_ref[...]

    @pl.when(i % tiles_per_seq == 0)
    def _():
        carry[j] = jnp.zeros(carry.shape[1:], F32)

    act_prev[...] = act_next[...]
    xn = xn_ref[...]
    for t in range(2 * n_sub):
        cs = slice(sub * (t % n_sub), sub * (t % n_sub + 1))
        bufs[t][pl.ds(0, SUBLANE), :] = carry[j, pl.ds(SUBLANE * (t // n_sub), SUBLANE), cs]

    def conv(w_ref, buf, cw_ref, cs):
        buf[pl.ds(SUBLANE, tm), :] = _dot(xn, w_ref[:, cs])
        y = cw_ref[k:k + 1, cs] + cw_ref[k - 1:k, cs] * buf[pl.ds(SUBLANE, tm), :]
        for d in range(1, k):
            y = y + cw_ref[k - 1 - d:k - d, cs] * buf[pl.ds(SUBLANE - d, tm), :]
        return y

    for t in range(n_sub):
        cs = slice(sub * t, sub * (t + 1))
        val = conv(wv_ref, bufs[t], cwv_ref, cs)
        gate = conv(wg_ref, bufs[n_sub + t], cwg_ref, cs)
        act_next[:, cs] = (val * jax.nn.gelu(gate)).astype(BF16)
    o_ref[...] += _dot(act_prev[...], wo_ref[...])

    for t in range(2 * n_sub):
        cs = slice(sub * (t % n_sub), sub * (t % n_sub + 1))
        carry[j, pl.ds(SUBLANE * (t // n_sub), SUBLANE), cs] = bufs[t][pl.ds(tm, SUBLANE), :]

    if final_norm:
        @pl.when((jp == n_j - 1) & (s > 0))
        def _():
            def norm(sl):
                o_ref[sl, :] = _rms_rows(o_ref[sl, :], fg_ref[...])
            row_blocks(norm)


def _ffn(x, g, w_in, cwb, w_out, fg, *, layer, seq, tm, tf, final_norm, sub=256):
    m, d = x.shape
    d_ff = w_out.shape[1]
    n_i = m // tm
    n_j = d_ff // tf
    assert n_j >= 2
    kb = cwb.shape[1]
    last = n_i * n_j - 1
    ci = lambda s: jnp.minimum(s, last) // n_j
    cj = lambda s: jnp.minimum(s, last) % n_j
    pi = lambda s: jnp.maximum(s - 1, 0) // n_j
    pj = lambda s: jnp.maximum(s - 1, 0) % n_j
    return pl.pallas_call(
        functools.partial(_ffn_body, rows=min(tm, 256), n_i=n_i, n_j=n_j, tiles_per_seq=seq // tm,
                          final_norm=final_norm, sub=sub),
        grid=(n_i * n_j + 1,),
        in_specs=[pl.BlockSpec((tm, d), lambda s: (ci(s), 0)),
                  pl.BlockSpec((1, d), lambda s: (0, 0)),
                  pl.BlockSpec((None, d, tf), lambda s: (layer, 0, cj(s))),
                  pl.BlockSpec((None, d, tf), lambda s: (layer, 0, n_j + cj(s))),
                  pl.BlockSpec((None, kb, tf), lambda s: (layer, 0, cj(s))),
                  pl.BlockSpec((None, kb, tf), lambda s: (layer, 0, n_j + cj(s))),
                  pl.BlockSpec((None, tf, d), lambda s: (layer, pj(s), 0)),
                  pl.BlockSpec((1, d), lambda s: (0, 0))],
        out_specs=pl.BlockSpec((tm, d), lambda s: (pi(s), 0)),
        out_shape=jax.ShapeDtypeStruct((m, d), F32),
        scratch_shapes=[pltpu.VMEM((tm, d), BF16),
                        pltpu.VMEM((tm, tf), BF16),
                        pltpu.VMEM((tm, tf), BF16),
                        pltpu.VMEM((n_j, 2 * SUBLANE, tf), F32)]
                       + [pltpu.VMEM((tm + SUBLANE, sub), F32) for _ in range(2 * tf // sub)],
        compiler_params=_cparams("arbitrary"),
        name="conv_ffn",
    )(x, g.reshape(1, d), w_in, w_in, cwb, cwb, w_out, fg.reshape(1, d))


def _pad_cols(w, n):
    return jnp.pad(w, ((0, 0), (0, n - w.shape[1])))


def _group_lanes(v, groups, heads):
    lead = v.shape[:-1]
    v = v.reshape(lead + (groups, heads))
    v = jnp.pad(v, [(0, 0)] * len(lead) + [(0, 0), (0, LANE - heads)])
    return v.reshape(lead + (groups * LANE,))


def kernel(x, mem, norm_mix_g, norm_cross_g, norm_ffn_g, ab_w_in, lru_conv_w, lru_conv_b, lru_w_a, lru_b_a, lru_w_x, lru_b_x, lru_lambda, ssd_conv_w, ssd_conv_b, ssd_dt_bias, ssd_a_log, ssd_d, ssd_norm_g, ab_w_out, gla_w_in, gla_w_gate_up, gla_b_gate, gla_norm_g, gla_w_out, mem_norm_g, xa_w_q, xa_w_kv, xa_w_o, ffn_w_in, ffn_conv_w, ffn_conv_b, ffn_w_out, final_norm_g):
    bsz, seq, d = x.shape
    mem_len = mem.shape[1]
    depth = norm_mix_g.shape[0]
    m = bsz * seq
    assert seq % 1024 == 0 and d % 512 == 0

    w_a = lru_lambda.shape[1]
    h_b = ssd_dt_bias.shape[1]
    w_b = ssd_norm_g.shape[1]
    p_b = w_b // h_b
    conv_dim = ssd_conv_w.shape[2]
    g_b = 4
    n_b = (conv_dim - w_b) // (2 * g_b)
    r_b = h_b // g_b
    assert ab_w_in.shape[2] == 2 * w_a + w_b + conv_dim + h_b and r_b * p_b == 512 and n_b == LANE
    dk_c = gla_b_gate.shape[1]
    dv_c = gla_norm_g.shape[1]
    rank = gla_w_gate_up.shape[1]
    h_c = 4
    assert gla_w_in.shape[2] == 2 * dk_c + 2 * dv_c + rank and rank <= LANE

    x2 = x.reshape(m, d)
    kv_in = mem.reshape(bsz * mem_len, d)
    expand = (lax.broadcasted_iota(jnp.int32, (LANE, r_b * p_b), 1) // p_b
              == lax.broadcasted_iota(jnp.int32, (LANE, r_b * p_b), 0)).astype(BF16)
    ab_w_in_bf, ab_w_out_bf = ab_w_in.astype(BF16), ab_w_out.astype(BF16)
    gla_w_in_bf, gla_w_out_bf = gla_w_in.astype(BF16), gla_w_out.astype(BF16)
    xa_w_q_bf, xa_w_kv_bf, xa_w_o_bf = xa_w_q.astype(BF16), xa_w_kv.astype(BF16), xa_w_o.astype(BF16)
    ffn_w_in_bf, ffn_w_out_bf = ffn_w_in.astype(BF16), ffn_w_out.astype(BF16)
    ffn_cwb = jnp.concatenate([ffn_conv_w, ffn_conv_b[:, None, :]], axis=1)

    for layer in range(depth):
        j = layer // 2
        if layer % 2 == 0:
            main = 2 * w_a + w_b + conv_dim
            assert main % 1024 == 0
            w_tail = _group_lanes(ab_w_in_bf[j][:, main:], g_b, r_b)
            proj, dt_proj = _norm_matmul(x2, norm_mix_g[layer], ab_w_in_bf, layer=j, tm=1024, tn=1024,
                                         out_dtype=F32, w_tail=w_tail)
            y_a = _rglru(proj, lru_conv_w[j], lru_conv_b[j], lru_w_a[j].astype(BF16), lru_b_a[j],
                         lru_w_x[j].astype(BF16), lru_b_x[j], lru_lambda[j],
                         bsz=bsz, seq=seq, width=w_a, gate_col=0, x_col=w_a, rows=1024, tc=512)
            y_b = _ssd(proj, dt_proj, ssd_conv_w[j], ssd_conv_b[j],
                       _group_lanes(ssd_dt_bias[j], g_b, r_b).reshape(1, -1),
                       _group_lanes(ssd_a_log[j], g_b, r_b).reshape(1, -1),
                       jnp.repeat(ssd_d[j], p_b).reshape(1, -1), ssd_norm_g[j].reshape(1, -1), expand,
                       bsz=bsz, seq=seq, groups=g_b, heads=r_b, hdim=p_b, nst=n_b,
                       z_col=2 * w_a, xs_col=2 * w_a + w_b, bm_col=2 * w_a + 2 * w_b,
                       cm_col=2 * w_a + 2 * w_b + g_b * n_b, rows=256)
            x2 = _matmul_res([y_a, y_b], [(ab_w_out_bf, j, 0), (ab_w_out_bf, j, 1)], x2, tm=512, tn=1024)
        else:
            main = 2 * dk_c + 2 * dv_c
            assert main % 1024 == 0
            proj, gl_proj = _norm_matmul(x2, norm_mix_g[layer], gla_w_in_bf, layer=j, tm=1024, tn=1024, out_dtype=F32,
                                         w_tail=_pad_cols(gla_w_in_bf[j][:, main:], LANE))
            wg = jnp.pad(gla_w_gate_up[j], ((0, LANE - rank), (0, 0))).astype(BF16)
            o = _gla(proj, gl_proj, wg, gla_b_gate[j].reshape(1, -1), gla_norm_g[j].reshape(1, -1),
                     bsz=bsz, seq=seq, heads=h_c, dk=dk_c // h_c, dv=dv_c // h_c,
                     q_col=0, k_col=dk_c, v_col=2 * dk_c, r_col=2 * dk_c + dv_c, rows=1024)
            x2 = _matmul_res([o], [(gla_w_out_bf, j, 0)], x2, tm=512, tn=d)

        q = _norm_matmul(x2, norm_cross_g[layer], xa_w_q_bf, layer=layer, tm=1024, tn=1024, out_dtype=BF16)
        kv = _norm_matmul(kv_in, mem_norm_g, xa_w_kv_bf, layer=layer, tm=bsz * mem_len, tn=1024, out_dtype=BF16)
        att = _xattn(q, kv, seq=seq, mem_len=mem_len, tm=512)
        x2 = _matmul_res([att], [(xa_w_o_bf, layer, 0)], x2, tm=512, tn=d)

        x2 = _ffn(x2, norm_ffn_g[layer], ffn_w_in_bf, ffn_cwb, ffn_w_out_bf, final_norm_g, layer=layer,
                  seq=seq, tm=512, tf=1024, final_norm=(layer == depth - 1))
    return x2.reshape(bsz, seq, d)
```
